```python
import jax, jax.numpy as jnp
from jax import lax
import numpy as np

D_MODEL = 1024
BATCH = 4
SEQ = 8192
DEPTH = 4

CHUNK = 64
MIX_DIM = D_MODEL
MLA_HEADS = 8
MLA_NOPE = 64
MLA_ROPE = 32
MLA_V = 64
Q_LORA = 384
KV_LORA = 256
Q_BLOCK = 128
ROPE_THETA = 10000.0
CA_HEADS = 8
CA_HEAD_DIM = 64
LEFT_CHUNKS = 8
REL_CLIP = 128
FFN_DIM = 3584
N_EXPERTS = 8
TOP_K = 2
EXPERT_BLOCK = 256
EPS = 1e-6
NEG_INF = -1e30

CA_DIM = CA_HEADS * CA_HEAD_DIM
IN_SPLITS = (Q_LORA, KV_LORA, MLA_ROPE, CA_DIM, CA_DIM, CA_DIM)
IN_DIM = sum(IN_SPLITS)
N_DENSE = (DEPTH + 1) // 2
N_MOE = DEPTH // 2

kernel_name = 'hybrid_mla_chunkattn_moe_encoder'


def rms_norm(x, g):
    xf = x.astype(jnp.float32)
    y = xf * lax.rsqrt(jnp.mean(xf * xf, axis=-1, keepdims=True) + EPS)
    return (y * g.astype(jnp.float32)).astype(x.dtype)


def rope_tables(seq, dim, dtype):
    inv = 1.0 / (ROPE_THETA ** (jnp.arange(0, dim, 2, dtype=jnp.float32) / dim))
    ang = jnp.arange(seq, dtype=jnp.float32)[:, None] * inv[None, :]
    return jnp.cos(ang).astype(dtype), jnp.sin(ang).astype(dtype)


def apply_rope(x, cos, sin):
    x1, x2 = jnp.split(x, 2, axis=-1)
    return jnp.concatenate([x1 * cos - x2 * sin, x1 * sin + x2 * cos], axis=-1)


def mla_attention(c_q, c_kv, k_r, q_norm, w_uq, kv_norm, w_ukv):
    B, S, _ = c_q.shape
    cos, sin = rope_tables(S, MLA_ROPE, c_q.dtype)
    q = (rms_norm(c_q, q_norm) @ w_uq).reshape(B, S, MLA_HEADS, MLA_NOPE + MLA_ROPE)
    q_nope = q[..., :MLA_NOPE]
    q_rope = apply_rope(q[..., MLA_NOPE:], cos[:, None, :], sin[:, None, :])
    kv = (rms_norm(c_kv, kv_norm) @ w_ukv).reshape(B, S, MLA_HEADS, MLA_NOPE + MLA_V)
    k_nope, v = kv[..., :MLA_NOPE], kv[..., MLA_NOPE:]
    k_rope = apply_rope(k_r, cos, sin)
    scale = (MLA_NOPE + MLA_ROPE) ** -0.5
    n_blk = S // Q_BLOCK
    k_chunk = jnp.arange(S) // CHUNK

    def to_blocks(t):
        return jnp.moveaxis(t.reshape(B, n_blk, Q_BLOCK, *t.shape[2:]), 1, 0)

    def one_block(args):
        qn, qr, start = args
        s = (jnp.einsum('bqhd,bkhd->bhqk', qn, k_nope)
             + jnp.einsum('bqhr,bkr->bhqk', qr, k_rope)).astype(jnp.float32) * scale
        q_chunk = (start + jnp.arange(Q_BLOCK)) // CHUNK
        mask = k_chunk[None, :] <= q_chunk[:, None]
        p = jax.nn.softmax(jnp.where(mask, s, NEG_INF), axis=-1).astype(v.dtype)
        return jnp.einsum('bhqk,bkhd->bqhd', p, v)

    out = lax.map(one_block, (to_blocks(q_nope), to_blocks(q_rope), jnp.arange(n_blk) * Q_BLOCK))
    return jnp.moveaxis(out, 0, 1).reshape(B, S, MLA_HEADS * MLA_V)


def chunk_attention(q, k, v, rel_bias):
    B, S, _ = q.shape
    n_chunks = S // CHUNK
    band = LEFT_CHUNKS + 1
    shp = (B, n_chunks, CHUNK, CA_HEADS, CA_HEAD_DIM)
    q, k, v = q.reshape(shp), k.reshape(shp), v.reshape(shp)
    pad = ((0, 0), (LEFT_CHUNKS, 0), (0, 0), (0, 0), (0, 0))
    band_idx = jnp.arange(n_chunks)[:, None] + jnp.arange(band)[None, :]
    band_shape = (B, n_chunks, band * CHUNK, CA_HEADS, CA_HEAD_DIM)
    kb = jnp.pad(k, pad)[:, band_idx].reshape(band_shape)
    vb = jnp.pad(v, pad)[:, band_idx].reshape(band_shape)
    valid = jnp.repeat(band_idx >= LEFT_CHUNKS, CHUNK, axis=1)
    q_pos = LEFT_CHUNKS * CHUNK + jnp.arange(CHUNK)
    k_pos = jnp.arange(band * CHUNK)
    rel = jnp.clip(q_pos[:, None] - k_pos[None, :], -REL_CLIP, REL_CLIP) + REL_CLIP
    bias = rel_bias[:, rel].astype(jnp.float32)
    s = jnp.einsum('bcqhd,bckhd->bchqk', q, kb).astype(jnp.float32) * (CA_HEAD_DIM ** -0.5) + bias
    s = jnp.where(valid[None, :, None, None, :], s, NEG_INF)
    p = jax.nn.softmax(s, axis=-1).astype(v.dtype)
    o = jnp.einsum('bchqk,bckhd->bcqhd', p, vb)
    return o.reshape(B, S, CA_DIM)


def swiglu(x, w1, w3, w2):
    return (jax.nn.silu(x @ w1) * (x @ w3)) @ w2


def moe_swiglu(x, w_router, w1, w3, w2):
    B, S, D = x.shape
    T = B * S
    n_assign = T * TOP_K
    xt = x.reshape(T, D)
    logits = (xt @ w_router).astype(jnp.float32)
    top_vals, top_idx = lax.top_k(logits, TOP_K)
    gates = jax.nn.softmax(top_vals, axis=-1)
    expert_ids = top_idx.reshape(-1).astype(jnp.int32)
    token_ids = (jnp.arange(n_assign) // TOP_K).astype(jnp.int32)
    gate_flat = gates.reshape(-1)
    order = jnp.argsort(expert_ids)
    sorted_e = expert_ids[order]
    counts = jnp.bincount(expert_ids, length=N_EXPERTS)
    padded = (counts + EXPERT_BLOCK - 1) // EXPERT_BLOCK * EXPERT_BLOCK
    starts = jnp.cumsum(counts) - counts
    pends = jnp.cumsum(padded)
    pstarts = pends - padded
    dest = pstarts[sorted_e] + (jnp.arange(n_assign) - starts[sorted_e])
    n_rows = n_assign + N_EXPERTS * EXPERT_BLOCK
    n_blocks = n_rows // EXPERT_BLOCK
    row_token = jnp.zeros((n_rows,), jnp.int32).at[dest].set(token_ids[order])
    row_gate = jnp.zeros((n_rows,), jnp.float32).at[dest].set(gate_flat[order])
    block_expert = jnp.minimum(
        jnp.searchsorted(pends, jnp.arange(n_blocks) * EXPERT_BLOCK, side='right'), N_EXPERTS - 1)
    xs = xt[row_token].reshape(n_blocks, EXPERT_BLOCK, D)

    def run(args):
        xb, e = args
        return swiglu(xb, w1[e], w3[e], w2[e])

    ys = lax.map(run, (xs, block_expert)).reshape(n_rows, D)
    ys = ys * row_gate[:, None].astype(ys.dtype)
    return jnp.zeros_like(xt).at[row_token].add(ys).reshape(B, S, D)


def setup_inputs(seed: int = 0) -> dict:
    key = jax.random.key(seed)
    ks = jax.random.split(key, 20)
    nrm = lambda k, shape, fan: jax.random.normal(k, shape, jnp.float32) * (fan ** -0.5)
    gain = lambda k, shape: 1.0 + 0.02 * jax.random.normal(k, shape, jnp.float32)
    return {
        'x': jax.random.normal(ks[0], (BATCH, SEQ, D_MODEL), jnp.float32),
        'attn_norm': gain(ks[1], (DEPTH, D_MODEL)),
        'w_in': nrm(ks[2], (DEPTH, D_MODEL, IN_DIM), D_MODEL),
        'q_norm': gain(ks[3], (DEPTH, Q_LORA)),
        'w_uq': nrm(ks[4], (DEPTH, Q_LORA, MLA_HEADS * (MLA_NOPE + MLA_ROPE)), Q_LORA),
        'kv_norm': gain(ks[5], (DEPTH, KV_LORA)),
        'w_ukv': nrm(ks[6], (DEPTH, KV_LORA, MLA_HEADS * (MLA_NOPE + MLA_V)), KV_LORA),
        'rel_bias': 0.1 * jax.random.normal(ks[7], (DEPTH, CA_HEADS, 2 * REL_CLIP + 1), jnp.float32),
        'w_out': nrm(ks[8], (DEPTH, MIX_DIM, D_MODEL), MIX_DIM),
        'ffn_norm': gain(ks[9], (DEPTH, D_MODEL)),
        'dense_w1': nrm(ks[10], (N_DENSE, D_MODEL, FFN_DIM), D_MODEL),
        'dense_w3': nrm(ks[11], (N_DENSE, D_MODEL, FFN_DIM), D_MODEL),
        'dense_w2': nrm(ks[12], (N_DENSE, FFN_DIM, D_MODEL), FFN_DIM),
        'w_router': nrm(ks[13], (N_MOE, D_MODEL, N_EXPERTS), D_MODEL),
        'moe_w1': nrm(ks[14], (N_MOE, N_EXPERTS, D_MODEL, FFN_DIM), D_MODEL),
        'moe_w3': nrm(ks[15], (N_MOE, N_EXPERTS, D_MODEL, FFN_DIM), D_MODEL),
        'moe_w2': nrm(ks[16], (N_MOE, N_EXPERTS, FFN_DIM, D_MODEL), FFN_DIM),
        'final_norm': gain(ks[17], (D_MODEL,)),
    }


def reference(x, attn_norm, w_in, q_norm, w_uq, kv_norm, w_ukv, rel_bias, w_out, ffn_norm,
              dense_w1, dense_w3, dense_w2, w_router, moe_w1, moe_w3, moe_w2, final_norm):
    split_points = [int(v) for v in np.cumsum(IN_SPLITS)[:-1]]
    for i in range(DEPTH):
        h = rms_norm(x, attn_norm[i])
        c_q, c_kv, k_r, q_b, k_b, v_b = jnp.split(h @ w_in[i], split_points, axis=-1)
        a = mla_attention(c_q, c_kv, k_r, q_norm[i], w_uq[i], kv_norm[i], w_ukv[i])
        b = chunk_attention(q_b, k_b, v_b, rel_bias[i])
        x = x + jnp.concatenate([a, b], axis=-1) @ w_out[i]
        h = rms_norm(x, ffn_norm[i])
        j = i // 2
        if i % 2 == 0:
            f = swiglu(h, dense_w1[j], dense_w3[j], dense_w2[j])
        else:
            f = moe_swiglu(h, w_router[j], moe_w1[j], moe_w3[j], moe_w2[j])
        x = x + f
    return rms_norm(x, final_norm)
```

```python
import functools

import jax
import jax.numpy as jnp
import numpy as np
from jax import lax
from jax.experimental import pallas as pl
from jax.experimental.pallas import tpu as pltpu

D_MODEL = 1024
CHUNK = 64
MLA_HEADS = 8
MLA_NOPE = 64
MLA_ROPE = 32
MLA_V = 64
Q_LORA = 384
KV_LORA = 256
ROPE_THETA = 10000.0
CA_HEADS = 8
CA_HEAD_DIM = 64
LEFT_CHUNKS = 8
REL_CLIP = 128
FFN_DIM = 3584
N_EXPERTS = 8
TOP_K = 2
EPS = 1e-6
NEG_INF = -1e30

CA_DIM = CA_HEADS * CA_HEAD_DIM
BAND = LEFT_CHUNKS + 1
HALF_ROPE = MLA_ROPE // 2
LANES = 128
VMEM_LIMIT = 56 * 1024 * 1024

TM_PROJ = 512
TQ_MLA = 512
TQ_CA = 256
CA_WIN = LEFT_CHUNKS * CHUNK + TQ_CA
TM_FFN = 512
TF_FFN = 512
TM_MOVE = 256

BF16 = jnp.bfloat16
F32 = jnp.float32


def _params(*sem):
    return pltpu.CompilerParams(dimension_semantics=sem, vmem_limit_bytes=VMEM_LIMIT)


def _rms(x, g):
    ms = jnp.mean(x * x, axis=-1, keepdims=True)
    return x * lax.rsqrt(ms + EPS) * g


def _dot(a, b):
    return jnp.dot(a, b, preferred_element_type=F32)


def _dot_nt(a, b):
    return lax.dot_general(a, b, (((1,), (1,)), ((), ())), preferred_element_type=F32)


def _pre_attn_kernel(x_ref, an_ref, wcq_ref, wckv_ref, wkr_ref, wqkvb_ref, qn_ref, wuq_ref,
                     kvn_ref, wuk_ref, wuv_ref, cq_ref, sq_ref, ck_ref, sk_ref,
                     q_out, k_out, v_out, qb_out, kb_out, vb_out):
    h = _rms(x_ref[...], an_ref[...]).astype(BF16)
    c_q = _dot(h, wcq_ref[...])
    c_kv = _dot(h, wckv_ref[...])
    kr2 = _dot(h, wkr_ref[...])
    qkvb = _dot(h, wqkvb_ref[...])
    qb_out[...] = (qkvb[:, :CA_DIM] * (CA_HEAD_DIM ** -0.5)).astype(BF16)
    kb_out[...] = qkvb[:, CA_DIM:2 * CA_DIM].astype(BF16)
    vb_out[...] = qkvb[:, 2 * CA_DIM:].astype(BF16)

    q2 = _dot(_rms(c_q, qn_ref[...]).astype(BF16), wuq_ref[...])
    cq, sq = cq_ref[...], sq_ref[...]
    width = MLA_HEADS * LANES
    for hd in range(MLA_HEADS):
        lo = hd * LANES
        q_out[hd] = (q2[:, lo:lo + LANES] * cq + q2[:, width + lo:width + lo + LANES] * sq).astype(BF16)

    ckvn = _rms(c_kv, kvn_ref[...]).astype(BF16)
    kn = _dot(ckvn, wuk_ref[...])
    vv = _dot(ckvn, wuv_ref[...])
    kr = kr2[:, :LANES] * ck_ref[...] + kr2[:, LANES:] * sk_ref[...]
    for hd in range(MLA_HEADS):
        k_out[hd] = (kn[:, hd * LANES:(hd + 1) * LANES] + kr).astype(BF16)
    for p in range(MLA_HEADS // 2):
        v_out[p] = vv[:, p * LANES:(p + 1) * LANES].astype(BF16)


def _pre_attn(x, lw, tabs):
    B, S, D = x.shape
    tm = TM_PROJ
    n_s = S // tm
    full = lambda a: pl.BlockSpec(a.shape, lambda b, i: (0,) * a.ndim)
    tab = pl.BlockSpec((tm, LANES), lambda b, i: (i, 0))
    tok = lambda w: pl.BlockSpec((None, tm, w), lambda b, i: (b, i, 0))
    heads = lambda n: pl.BlockSpec((None, n, tm, LANES), lambda b, i: (b, 0, i, 0))
    weights = [lw['attn_norm'], lw['w_cq'], lw['w_ckv'], lw['w_kr'], lw['w_qkvb'], lw['q_norm'],
               lw['w_uq'], lw['kv_norm'], lw['w_uk'], lw['w_uv']]
    return pl.pallas_call(
        _pre_attn_kernel,
        grid=(B, n_s),
        in_specs=[tok(D)] + [full(w) for w in weights] + [tab] * 4,
        out_specs=[heads(MLA_HEADS), heads(MLA_HEADS), heads(MLA_HEADS // 2),
                   tok(CA_DIM), tok(CA_DIM), tok(CA_DIM)],
        out_shape=[jax.ShapeDtypeStruct((B, MLA_HEADS, S, LANES), BF16),
                   jax.ShapeDtypeStruct((B, MLA_HEADS, S, LANES), BF16),
                   jax.ShapeDtypeStruct((B, MLA_HEADS // 2, S, LANES), BF16),
                   jax.ShapeDtypeStruct((B, S, CA_DIM), BF16),
                   jax.ShapeDtypeStruct((B, S, CA_DIM), BF16),
                   jax.ShapeDtypeStruct((B, S, CA_DIM), BF16)],
        compiler_params=_params("parallel", "parallel"),
        name="pre_attn",
    )(x, *weights, *tabs)


def _mla_kernel(qi_ref, kj_ref, q_ref, k_ref, v_ref, o_ref, m_scr, l_scr, acc_scr, *, tq):
    t = pl.program_id(1)
    qi = qi_ref[t]
    kj = kj_ref[t]

    @pl.when(kj == 0)
    def _():
        m_scr[...] = jnp.full(m_scr.shape, NEG_INF, F32)
        l_scr[...] = jnp.zeros(l_scr.shape, F32)
        acc_scr[...] = jnp.zeros(acc_scr.shape, F32)

    def step(masked):
        def head(hd, carry):
            s = _dot_nt(q_ref[hd], k_ref[hd])
            if masked:
                row = lax.broadcasted_iota(jnp.int32, s.shape, 0) // CHUNK
                col = lax.broadcasted_iota(jnp.int32, s.shape, 1) // CHUNK
                s = jnp.where(col <= row, s, NEG_INF)
            m_prev = m_scr[hd]
            l_prev = l_scr[hd]
            m_new = jnp.maximum(m_prev, jnp.max(s, axis=1, keepdims=True))
            alpha = jnp.exp(m_prev - m_new)
            p = jnp.exp(s - jnp.tile(m_new, (1, tq // LANES)))
            l_scr[hd] = alpha * l_prev + jnp.sum(p, axis=1, keepdims=True)
            m_scr[hd] = m_new
            acc_scr[hd] = acc_scr[hd] * alpha + _dot(p.astype(BF16), v_ref[hd // 2])
            return carry
        lax.fori_loop(0, MLA_HEADS, head, 0)

    @pl.when(kj < qi)
    def _():
        step(False)

    @pl.when(kj == qi)
    def _():
        step(True)
        lane = lax.broadcasted_iota(jnp.int32, (tq, LANES), 1)
        for p in range(MLA_HEADS // 2):
            even = acc_scr[2 * p] / l_scr[2 * p]
            odd = acc_scr[2 * p + 1] / l_scr[2 * p + 1]
            o_ref[:, p * LANES:(p + 1) * LANES] = jnp.where(lane < MLA_V, even, odd).astype(BF16)


def _mla_attention(q, k, v):
    B, H, S, _ = q.shape
    tq = TQ_MLA
    n = S // tq
    pairs = [(i, j) for i in range(n) for j in range(i + 1)]
    qi = jnp.asarray(np.array([p[0] for p in pairs], np.int32))
    kj = jnp.asarray(np.array([p[1] for p in pairs], np.int32))
    grid_spec = pltpu.PrefetchScalarGridSpec(
        num_scalar_prefetch=2,
        grid=(B, len(pairs)),
        in_specs=[pl.BlockSpec((None, H, tq, LANES), lambda b, t, qi, kj: (b, 0, qi[t], 0)),
                  pl.BlockSpec((None, H, tq, LANES), lambda b, t, qi, kj: (b, 0, kj[t], 0)),
                  pl.BlockSpec((None, H // 2, tq, LANES), lambda b, t, qi, kj: (b, 0, kj[t], 0))],
        out_specs=pl.BlockSpec((None, tq, H * MLA_V), lambda b, t, qi, kj: (b, qi[t], 0)),
        scratch_shapes=[pltpu.VMEM((H, tq, LANES), F32)] * 3,
    )
    return pl.pallas_call(
        functools.partial(_mla_kernel, tq=tq),
        grid_spec=grid_spec,
        out_shape=jax.ShapeDtypeStruct((B, S, H * MLA_V), BF16),
        compiler_params=_params("parallel", "arbitrary"),
        name="mla_attention",
    )(qi, kj, q, k, v)


def _rel_bias_kernel(tab_ref, o_ref):
    r = lax.broadcasted_iota(jnp.int32, (CHUNK, BAND * CHUNK), 0)
    c = lax.broadcasted_iota(jnp.int32, (CHUNK, BAND * CHUNK), 1)
    rel = jnp.clip(LEFT_CHUNKS * CHUNK + r - c, -REL_CLIP, REL_CLIP) + REL_CLIP
    for hd in range(CA_HEADS):
        def body(t, acc):
            return jnp.where(rel == t, tab_ref[hd, t], acc)
        o_ref[hd] = lax.fori_loop(0, 2 * REL_CLIP + 1, body, jnp.zeros(rel.shape, F32))


def _rel_bias_band(rel_bias):
    depth = rel_bias.shape[0]
    return pl.pallas_call(
        _rel_bias_kernel,
        grid=(depth,),
        in_specs=[pl.BlockSpec((None, CA_HEADS, 2 * REL_CLIP + 1), lambda d: (d, 0, 0),
                               memory_space=pltpu.SMEM)],
        out_specs=pl.BlockSpec((None, CA_HEADS, CHUNK, BAND * CHUNK), lambda d: (d, 0, 0, 0)),
        out_shape=jax.ShapeDtypeStruct((depth, CA_HEADS, CHUNK, BAND * CHUNK), F32),
        compiler_params=_params("parallel"),
        name="rel_bias_band",
    )(rel_bias)


def _bias_tiles(band):
    depth = band.shape[0]
    n_qc = TQ_CA // CHUNK
    n_kc = CA_WIN // CHUNK
    neg = jnp.full((depth, CA_HEADS, CHUNK, CHUNK), NEG_INF, F32)
    rows = []
    for rc in range(n_qc):
        blocks = []
        for cc in range(n_kc):
            j = cc - rc
            blocks.append(band[..., j * CHUNK:(j + 1) * CHUNK] if 0 <= j < BAND else neg)
        rows.append(jnp.concatenate(blocks, axis=-1))
    return jnp.concatenate(rows, axis=-2)


def _chunk_attn_kernel(q_ref, k2_ref, k1_ref, k0_ref, v2_ref, v1_ref, v0_ref, bias_ref, o_ref):
    i = pl.program_id(1)
    tq, win = TQ_CA, CA_WIN
    col = lax.broadcasted_iota(jnp.int32, (tq, win), 1)
    exists = col >= LEFT_CHUNKS * CHUNK - i * tq
    lane_q = lax.broadcasted_iota(jnp.int32, (tq, LANES), 1)
    for p in range(CA_HEADS // 2):
        sl = slice(p * LANES, (p + 1) * LANES)
        q = q_ref[:, sl]
        k = jnp.concatenate([k2_ref[:, sl], k1_ref[:, sl], k0_ref[:, sl]], axis=0)
        v = jnp.concatenate([v2_ref[:, sl], v1_ref[:, sl], v0_ref[:, sl]], axis=0)
        outs = []
        for half in range(2):
            keep = (lane_q < CA_HEAD_DIM) if half == 0 else (lane_q >= CA_HEAD_DIM)
            s = _dot_nt(jnp.where(keep, q, jnp.zeros_like(q)), k) + bias_ref[2 * p + half]
            s = jnp.where(exists, s, NEG_INF)
            m = jnp.max(s, axis=1, keepdims=True)
            e = jnp.exp(s - m)
            l = jnp.sum(e, axis=1, keepdims=True)
            outs.append(_dot(e.astype(BF16), v) / l)
        o_ref[:, sl] = jnp.where(lane_q < CA_HEAD_DIM, outs[0], outs[1]).astype(BF16)


def _chunk_attention(qb, kb, vb, bias):
    B, S, _ = qb.shape
    tq = TQ_CA
    n = S // tq
    cur = pl.BlockSpec((None, tq, CA_DIM), lambda b, i: (b, i, 0))
    prev1 = pl.BlockSpec((None, tq, CA_DIM), lambda b, i: (b, jnp.maximum(i - 1, 0), 0))
    prev2 = pl.BlockSpec((None, tq, CA_DIM), lambda b, i: (b, jnp.maximum(i - 2, 0), 0))
    return pl.pallas_call(
        _chunk_attn_kernel,
        grid=(B, n),
        in_specs=[cur, prev2, prev1, cur, prev2, prev1, cur,
                  pl.BlockSpec(bias.shape, lambda b, i: (0, 0, 0))],
        out_specs=cur,
        out_shape=jax.ShapeDtypeStruct((B, S, CA_DIM), BF16),
        compiler_params=_params("parallel", "parallel"),
        name="chunk_attention",
    )(qb, kb, kb, kb, vb, vb, vb, bias)


def _post_attn_kernel(x_ref, a_ref, b_ref, woa_ref, wob_ref, fn_ref, *rest, route):
    if route:
        wr_ref, x2_out, h_out, meta_out, gate_out, cnt_out, run_scr = rest
    else:
        x2_out, h_out = rest
    x2 = x_ref[...] + _dot(a_ref[...], woa_ref[...]) + _dot(b_ref[...], wob_ref[...])
    x2_out[...] = x2
    h = _rms(x2, fn_ref[...])
    h_out[...] = h.astype(h_out.dtype)
    if not route:
        return

    tm = x2.shape[0]
    step = pl.program_id(0)

    @pl.when(step == 0)
    def _():
        run_scr[...] = jnp.zeros(run_scr.shape, F32)

    lane = lax.broadcasted_iota(jnp.int32, (tm, LANES), 1)
    logits = jnp.dot(h, wr_ref[...], preferred_element_type=F32, precision=lax.Precision.HIGHEST)
    logits = jnp.where(lane < N_EXPERTS, logits, -jnp.inf)
    v1 = jnp.max(logits, axis=1, keepdims=True)
    e1 = jnp.min(jnp.where(logits == v1, lane, LANES), axis=1, keepdims=True)
    rest_logits = jnp.where(lane == e1, -jnp.inf, logits)
    v2 = jnp.max(rest_logits, axis=1, keepdims=True)
    e2 = jnp.min(jnp.where(rest_logits == v2, lane, LANES), axis=1, keepdims=True)
    w2 = jnp.exp(v2 - v1)
    g1 = 1.0 / (1.0 + w2)
    g2 = w2 / (1.0 + w2)

    sel1 = lane == e1
    sel2 = lane == e2
    cnt = jnp.where(sel1 | sel2, 1.0, 0.0)
    r_i = lax.broadcasted_iota(jnp.int32, (tm, tm), 0)
    c_i = lax.broadcasted_iota(jnp.int32, (tm, tm), 1)
    lower = jnp.where(c_i < r_i, 1.0, 0.0).astype(BF16)
    before = _dot(lower, cnt.astype(BF16)) + run_scr[...]
    rank1 = jnp.sum(jnp.where(sel1, before, 0.0), axis=1, keepdims=True)
    rank2 = jnp.sum(jnp.where(sel2, before, 0.0), axis=1, keepdims=True)
    run_scr[...] = run_scr[...] + jnp.sum(cnt, axis=0, keepdims=True)
    cnt_out[...] = run_scr[...]

    meta = jnp.where(lane == 0, e1, jnp.where(lane == 1, e2, jnp.where(
        lane == 2, rank1.astype(jnp.int32), jnp.where(lane == 3, rank2.astype(jnp.int32), 0))))
    meta_out[...] = meta
    gate_out[...] = jnp.where(lane == 0, g1, jnp.where(lane == 1, g2, 0.0))


def _post_attn(x, a, b, lw, route):
    T, D = x.shape
    tm = TM_PROJ
    half = a.shape[1]
    tok = lambda w: pl.BlockSpec((tm, w), lambda i: (i, 0))
    full = lambda arr: pl.BlockSpec(arr.shape, lambda i: (0,) * arr.ndim)
    ins = [x, a, b, lw['w_out_a'], lw['w_out_b'], lw['ffn_norm']]
    in_specs = [tok(D), tok(half), tok(half), full(ins[3]), full(ins[4]), full(ins[5])]
    out_specs = [tok(D), tok(D)]
    out_shape = [jax.ShapeDtypeStruct((T, D), F32),
                 jax.ShapeDtypeStruct((T, D), F32 if route else BF16)]
    scratch = []
    if route:
        ins.append(lw['w_router'])
        in_specs.append(full(lw['w_router']))
        out_specs += [tok(LANES), tok(LANES), pl.BlockSpec((1, LANES), lambda i: (0, 0))]
        out_shape += [jax.ShapeDtypeStruct((T, LANES), jnp.int32),
                      jax.ShapeDtypeStruct((T, LANES), F32),
                      jax.ShapeDtypeStruct((1, LANES), F32)]
        scratch = [pltpu.VMEM((1, LANES), F32)]
    return pl.pallas_call(
        functools.partial(_post_attn_kernel, route=route),
        grid=(T // tm,),
        in_specs=in_specs,
        out_specs=out_specs,
        out_shape=out_shape,
        scratch_shapes=scratch,
        compiler_params=_params("arbitrary"),
        name="post_attn_route" if route else "post_attn",
    )(*ins)


def _ffn_kernel(be_ref, nv_ref, h_ref, *rest, residual):
    if residual:
        res_ref, w1_ref, w3_ref, w2_ref, o_ref = rest
    else:
        w1_ref, w3_ref, w2_ref, o_ref = rest
    i = pl.program_id(0)
    f = pl.program_id(1)

    @pl.when(f == 0)
    def _():
        o_ref[...] = res_ref[...] if residual else jnp.zeros(o_ref.shape, F32)

    @pl.when(i < nv_ref[0])
    def _():
        h = h_ref[...].astype(BF16)
        a = _dot(h, w1_ref[...])
        g = _dot(h, w3_ref[...])
        act = (a * jax.nn.sigmoid(a) * g).astype(BF16)
        o_ref[...] += _dot(act, w2_ref[...])


def _ffn(h, res, w1, w3, w2, block_expert, n_valid):
    R, D = h.shape
    tm, tf = TM_FFN, TF_FFN
    n_f = FFN_DIM // tf
    last = lambda i, nv: jnp.minimum(i, nv[0] - 1)
    row = pl.BlockSpec((tm, D), lambda i, f, be, nv: (last(i, nv), 0))
    out = pl.BlockSpec((tm, D), lambda i, f, be, nv: (i, 0))
    fcl = lambda i, f, nv: jnp.where(i < nv[0], f, n_f - 1)
    w13 = pl.BlockSpec((None, D, tf), lambda i, f, be, nv: (be[last(i, nv)], 0, fcl(i, f, nv)))
    w2s = pl.BlockSpec((None, tf, D), lambda i, f, be, nv: (be[last(i, nv)], fcl(i, f, nv), 0))
    residual = res is not None
    ins = [h] + ([res] if residual else []) + [w1, w3, w2]
    in_specs = [row] + ([out] if residual else []) + [w13, w13, w2s]
    return pl.pallas_call(
        functools.partial(_ffn_kernel, residual=residual),
        grid_spec=pltpu.PrefetchScalarGridSpec(
            num_scalar_prefetch=2, grid=(R // tm, n_f), in_specs=in_specs, out_specs=out),
        out_shape=jax.ShapeDtypeStruct((R, D), F32),
        compiler_params=_params("parallel", "arbitrary"),
        name="swiglu_res" if residual else "swiglu_expert",
    )(block_expert, n_valid, *ins)


def _dispatch_kernel(dest_ref, h_hbm, xs_in, xs_out, sem):
    del xs_in
    base = pl.program_id(0) * TM_MOVE

    def copy(r, k):
        return pltpu.make_async_copy(h_hbm.at[pl.ds(base + r, 1)],
                                     xs_out.at[pl.ds(dest_ref[TOP_K * r + k], 1)], sem)

    def start(r, c):
        for k in range(TOP_K):
            copy(r, k).start()
        return c

    def wait(r, c):
        for k in range(TOP_K):
            copy(r, k).wait()
        return c

    lax.fori_loop(0, TM_MOVE, start, 0)
    lax.fori_loop(0, TM_MOVE, wait, 0)


def _dispatch(h, dest_flat, n_rows):
    T, D = h.shape
    zeros = jnp.zeros((n_rows, D), h.dtype)
    return pl.pallas_call(
        _dispatch_kernel,
        grid=(T // TM_MOVE,),
        in_specs=[pl.BlockSpec((TOP_K * TM_MOVE,), lambda i: (i,), memory_space=pltpu.SMEM),
                  pl.BlockSpec(memory_space=pl.ANY),
                  pl.BlockSpec(memory_space=pl.ANY)],
        out_specs=pl.BlockSpec(memory_space=pl.ANY),
        out_shape=jax.ShapeDtypeStruct((n_rows, D), h.dtype),
        scratch_shapes=[pltpu.SemaphoreType.DMA(())],
        input_output_aliases={2: 0},
        compiler_params=_params("arbitrary"),
        name="moe_dispatch",
    )(dest_flat, h, zeros)


def _combine_kernel(dest_ref, x_ref, gate_ref, g_ref, ys_hbm, o_ref, buf, sem, *, final):
    tm = TM_MOVE

    def copy(r, k):
        return pltpu.make_async_copy(ys_hbm.at[pl.ds(dest_ref[TOP_K * r + k], 1)],
                                     buf.at[k, pl.ds(r, 1)], sem)

    def start(r, c):
        for k in range(TOP_K):
            copy(r, k).start()
        return c

    def wait(r, c):
        for k in range(TOP_K):
            copy(r, k).wait()
        return c

    lax.fori_loop(0, tm, start, 0)
    lax.fori_loop(0, tm, wait, 0)
    gates = gate_ref[...]
    y = x_ref[...] + (buf[0] * gates[:, 0:1] + buf[1] * gates[:, 1:2])
    o_ref[...] = _rms(y, g_ref[...]) if final else y


def _combine(x, ys, gates, dest_flat, final_g):
    T, D = x.shape
    tm = TM_MOVE
    final = final_g is not None
    g = final_g if final else jnp.ones((1, D), F32)
    return pl.pallas_call(
        functools.partial(_combine_kernel, final=final),
        grid=(T // tm,),
        in_specs=[pl.BlockSpec((TOP_K * tm,), lambda i: (i,), memory_space=pltpu.SMEM),
                  pl.BlockSpec((tm, D), lambda i: (i, 0)),
                  pl.BlockSpec((tm, LANES), lambda i: (i, 0)),
                  pl.BlockSpec((1, D), lambda i: (0, 0)),
                  pl.BlockSpec(memory_space=pl.ANY)],
        out_specs=pl.BlockSpec((tm, D), lambda i: (i, 0)),
        out_shape=jax.ShapeDtypeStruct((T, D), F32),
        scratch_shapes=[pltpu.VMEM((TOP_K, tm, D), F32), pltpu.SemaphoreType.DMA(())],
        compiler_params=_params("arbitrary"),
        name="moe_combine_final" if final else "moe_combine",
    )(dest_flat, x, gates, g, ys)


def _final_norm_kernel(x_ref, g_ref, o_ref):
    o_ref[...] = _rms(x_ref[...], g_ref[...])


def _final_norm(x, g):
    T, D = x.shape
    tm = TM_PROJ
    return pl.pallas_call(
        _final_norm_kernel,
        grid=(T // tm,),
        in_specs=[pl.BlockSpec((tm, D), lambda i: (i, 0)), pl.BlockSpec((1, D), lambda i: (0, 0))],
        out_specs=pl.BlockSpec((tm, D), lambda i: (i, 0)),
        out_shape=jax.ShapeDtypeStruct((T, D), F32),
        compiler_params=_params("parallel"),
        name="final_norm",
    )(x, g)


def _rope_tables(seq):
    inv = 1.0 / (ROPE_THETA ** (jnp.arange(0, MLA_ROPE, 2, dtype=F32) / MLA_ROPE))
    ang = jnp.arange(seq, dtype=F32)[:, None] * inv[None, :]
    cos, sin = jnp.cos(ang), jnp.sin(ang)
    ones = jnp.ones((seq, MLA_NOPE), F32)
    z_nope = jnp.zeros((seq, MLA_NOPE), F32)
    z_pad = jnp.zeros((seq, LANES - MLA_NOPE - MLA_ROPE), F32)
    scale = (MLA_NOPE + MLA_ROPE) ** -0.5
    cq = jnp.concatenate([ones, cos, cos, z_pad], axis=1) * scale
    sq = jnp.concatenate([z_nope, sin, sin, z_pad], axis=1) * scale
    ck = jnp.concatenate([z_nope, cos, cos, z_pad], axis=1)
    sk = jnp.concatenate([z_nope, sin, sin, z_pad], axis=1)
    return cq, sq, ck, sk


def _layer_weights(i, attn_norm, w_in, q_norm, w_uq, kv_norm, w_ukv, w_out, ffn_norm):
    w = w_in[i]
    o = np.cumsum([0, Q_LORA, KV_LORA, MLA_ROPE])
    w_cq, w_ckv, w_kr = (w[:, o[j]:o[j + 1]] for j in range(3))
    w_qkvb = w[:, o[3]:]
    d = w.shape[0]
    x1, x2 = w_kr[:, :HALF_ROPE], w_kr[:, HALF_ROPE:]
    z_nope = jnp.zeros((d, MLA_NOPE), F32)
    z_pad = jnp.zeros((d, LANES - MLA_NOPE - MLA_ROPE), F32)
    w_kr2 = jnp.concatenate([z_nope, x1, x2, z_pad, z_nope, -x2, x1, z_pad], axis=1)

    wq = w_uq[i].reshape(Q_LORA, MLA_HEADS, MLA_NOPE + MLA_ROPE)
    nope, q1, q2 = wq[..., :MLA_NOPE], wq[..., MLA_NOPE:MLA_NOPE + HALF_ROPE], wq[..., MLA_NOPE + HALF_ROPE:]
    zq_nope = jnp.zeros((Q_LORA, MLA_HEADS, MLA_NOPE), F32)
    zq_pad = jnp.zeros((Q_LORA, MLA_HEADS, LANES - MLA_NOPE - MLA_ROPE), F32)
    pre = jnp.concatenate([nope, q1, q2, zq_pad], axis=-1).reshape(Q_LORA, MLA_HEADS * LANES)
    swap = jnp.concatenate([zq_nope, -q2, q1, zq_pad], axis=-1).reshape(Q_LORA, MLA_HEADS * LANES)

    wkv = w_ukv[i].reshape(KV_LORA, MLA_HEADS, MLA_NOPE + MLA_V)
    zk = jnp.zeros((KV_LORA, MLA_HEADS, LANES - MLA_NOPE), F32)
    w_uk = jnp.concatenate([wkv[..., :MLA_NOPE], zk], axis=-1).reshape(KV_LORA, MLA_HEADS * LANES)
    w_uv = wkv[..., MLA_NOPE:].reshape(KV_LORA, MLA_HEADS * MLA_V)
    half = MLA_HEADS * MLA_V
    return {
        'attn_norm': attn_norm[i][None], 'w_cq': w_cq.astype(BF16), 'w_ckv': w_ckv.astype(BF16),
        'w_kr': w_kr2.astype(BF16), 'w_qkvb': w_qkvb.astype(BF16), 'q_norm': q_norm[i][None],
        'w_uq': jnp.concatenate([pre, swap], axis=1).astype(BF16), 'kv_norm': kv_norm[i][None],
        'w_uk': w_uk.astype(BF16), 'w_uv': w_uv.astype(BF16),
        'w_out_a': w_out[i][:half].astype(BF16), 'w_out_b': w_out[i][half:].astype(BF16),
        'ffn_norm': ffn_norm[i][None],
    }


def _moe_layout(meta, counts):
    T = meta.shape[0]
    blk = TM_FFN
    cnt = counts[0, :N_EXPERTS].astype(jnp.int32)
    padded = (cnt + blk - 1) // blk * blk
    pends = jnp.cumsum(padded)
    pstarts = pends - padded
    dest = pstarts[meta[:, :TOP_K]] + meta[:, TOP_K:2 * TOP_K]
    n_rows = T * TOP_K + N_EXPERTS * blk
    n_blocks = n_rows // blk
    block_expert = jnp.minimum(
        jnp.searchsorted(pends, jnp.arange(n_blocks, dtype=jnp.int32) * blk, side='right'),
        N_EXPERTS - 1).astype(jnp.int32)
    n_valid = (pends[-1:] // blk).astype(jnp.int32)
    return dest.reshape(-1).astype(jnp.int32), block_expert, n_valid, n_rows


def kernel(x, attn_norm, w_in, q_norm, w_uq, kv_norm, w_ukv, rel_bias, w_out, ffn_norm, dense_w1,
           dense_w3, dense_w2, w_router, moe_w1, moe_w3, moe_w2, final_norm):
    B, S, D = x.shape
    T = B * S
    depth = w_in.shape[0]
    tabs = _rope_tables(S)
    bias = _bias_tiles(_rel_bias_band(rel_bias))
    dense_be = jnp.zeros((T // TM_FFN,), jnp.int32)
    dense_nv = jnp.full((1,), T // TM_FFN, jnp.int32)
    out = None
    for i in range(depth):
        lw = _layer_weights(i, attn_norm, w_in, q_norm, w_uq, kv_norm, w_ukv, w_out, ffn_norm)
        q, k, v, qb, kb, vb = _pre_attn(x, lw, tabs)
        a = _mla_attention(q, k, v)
        b = _chunk_attention(qb, kb, vb, bias[i])
        xt = x.reshape(T, D)
        a2, b2 = a.reshape(T, -1), b.reshape(T, -1)
        j = i // 2
        last = i == depth - 1
        if i % 2 == 0:
            x2, h = _post_attn(xt, a2, b2, lw, route=False)
            y = _ffn(h, x2, dense_w1[j][None].astype(BF16), dense_w3[j][None].astype(BF16),
                     dense_w2[j][None].astype(BF16), dense_be, dense_nv)
            if last:
                out = _final_norm(y, final_norm[None])
        else:
            router = jnp.zeros((D, LANES), F32).at[:, :N_EXPERTS].set(w_router[j])
            lw['w_router'] = router
            x2, h, meta, gates, counts = _post_attn(xt, a2, b2, lw, route=True)
            dest, block_expert, n_valid, n_rows = _moe_layout(meta, counts)
            xs = _dispatch(h, dest, n_rows)
            ys = _ffn(xs, None, moe_w1[j].astype(BF16), moe_w3[j].astype(BF16),
                      moe_w2[j].astype(BF16), block_expert, n_valid)
            y = _combine(x2, ys, gates, dest, final_norm[None] if last else None)
            if last:
                out = y
        x = y.reshape(B, S, D)
    return out.reshape(B, S, D)
```

```python
import functools

import jax
import jax.numpy as jnp
import numpy as np
from jax import lax
from jax.experimental import pallas as pl
from jax.experimental.pallas import tpu as pltpu

D_MODEL = 1024
CHUNK = 64
MLA_HEADS = 8
MLA_NOPE = 64
MLA_ROPE = 32
MLA_V = 64
Q_LORA = 384
KV_LORA = 256
ROPE_THETA = 10000.0
CA_HEADS = 8
CA_HEAD_DIM = 64
LEFT_CHUNKS = 8
REL_CLIP = 128
FFN_DIM = 3584
N_EXPERTS = 8
TOP_K = 2
EPS = 1e-6
NEG_INF = -1e30

CA_DIM = CA_HEADS * CA_HEAD_DIM
BAND = LEFT_CHUNKS + 1
HALF_ROPE = MLA_ROPE // 2
LANES = 128
VMEM_LIMIT = 56 * 1024 * 1024

TM_PROJ = 512
TQ_MLA = 512
TQ_CA = 256
CA_WIN = LEFT_CHUNKS * CHUNK + TQ_CA
N_CA_LEAD = LEFT_CHUNKS * CHUNK // TQ_CA
TM_FFN = 512
TF_FFN = 1792
TM_DISPATCH = 512
TM_MOVE = 256
MOVE_UNROLL = 8

BF16 = jnp.bfloat16
F32 = jnp.float32


def _params(*sem):
    return pltpu.CompilerParams(dimension_semantics=sem, vmem_limit_bytes=VMEM_LIMIT)


def _rms(x, g):
    ms = jnp.mean(x * x, axis=-1, keepdims=True)
    return x * lax.rsqrt(ms + EPS) * g


def _dot(a, b):
    return jnp.dot(a, b, preferred_element_type=F32)


def _dot_nt(a, b):
    return lax.dot_general(a, b, (((1,), (1,)), ((), ())), preferred_element_type=F32)


def _pre_attn_kernel(x_ref, an_ref, wcq_ref, wckv_ref, wkr_ref, wqkvb_ref, qn_ref, wuq_ref,
                     kvn_ref, wuk_ref, wuv_ref, cq_ref, sq_ref, ck_ref, sk_ref,
                     q_out, k_out, v_out, qb_out, kb_out, vb_out):
    h = _rms(x_ref[...], an_ref[...]).astype(BF16)
    c_q = _dot(h, wcq_ref[...])
    c_kv = _dot(h, wckv_ref[...])
    kr2 = _dot(h, wkr_ref[...])
    qkvb = _dot(h, wqkvb_ref[...])
    qb_out[...] = (qkvb[:, :CA_DIM] * (CA_HEAD_DIM ** -0.5 * np.log2(np.e))).astype(BF16)
    kb_out[...] = qkvb[:, CA_DIM:2 * CA_DIM].astype(BF16)
    vb_out[...] = qkvb[:, 2 * CA_DIM:].astype(BF16)

    q2 = _dot(_rms(c_q, qn_ref[...]).astype(BF16), wuq_ref[...])
    cq, sq = cq_ref[...], sq_ref[...]
    width = MLA_HEADS * LANES
    for hd in range(MLA_HEADS):
        lo = hd * LANES
        q_out[hd] = (q2[:, lo:lo + LANES] * cq + q2[:, width + lo:width + lo + LANES] * sq).astype(BF16)

    ckvn = _rms(c_kv, kvn_ref[...]).astype(BF16)
    kn = _dot(ckvn, wuk_ref[...])
    vv = _dot(ckvn, wuv_ref[...])
    kr = kr2[:, :LANES] * ck_ref[...] + kr2[:, LANES:] * sk_ref[...]
    lane = lax.broadcasted_iota(jnp.int32, kr.shape, 1)
    one = jnp.where(lane == MLA_V, 1.0, 0.0)
    for hd in range(MLA_HEADS):
        k_out[hd] = (kn[:, hd * LANES:(hd + 1) * LANES] + kr).astype(BF16)
        v_out[hd] = (vv[:, hd * LANES:(hd + 1) * LANES] + one).astype(BF16)


def _pre_attn(x, lw, tabs):
    B, S, D = x.shape
    tm = TM_PROJ
    n_s = S // tm
    full = lambda a: pl.BlockSpec(a.shape, lambda b, i: (0,) * a.ndim)
    tab = pl.BlockSpec((tm, LANES), lambda b, i: (i, 0))
    tok = lambda w: pl.BlockSpec((None, tm, w), lambda b, i: (b, i, 0))
    heads = lambda n: pl.BlockSpec((None, n, tm, LANES), lambda b, i: (b, 0, i, 0))
    weights = [lw['attn_norm'], lw['w_cq'], lw['w_ckv'], lw['w_kr'], lw['w_qkvb'], lw['q_norm'],
               lw['w_uq'], lw['kv_norm'], lw['w_uk'], lw['w_uv']]
    return pl.pallas_call(
        _pre_attn_kernel,
        grid=(B, n_s),
        in_specs=[tok(D)] + [full(w) for w in weights] + [tab] * 4,
        out_specs=[heads(MLA_HEADS), heads(MLA_HEADS), heads(MLA_HEADS),
                   tok(CA_DIM), tok(CA_DIM), tok(CA_DIM)],
        out_shape=[jax.ShapeDtypeStruct((B, MLA_HEADS, S, LANES), BF16),
                   jax.ShapeDtypeStruct((B, MLA_HEADS, S, LANES), BF16),
                   jax.ShapeDtypeStruct((B, MLA_HEADS, S, LANES), BF16),
                   jax.ShapeDtypeStruct((B, S, CA_DIM), BF16),
                   jax.ShapeDtypeStruct((B, S, CA_DIM), BF16),
                   jax.ShapeDtypeStruct((B, S, CA_DIM), BF16)],
        compiler_params=_params("parallel", "parallel"),
        name="pre_attn",
    )(x, *weights, *tabs)


def _mla_kernel(qi_ref, kj_ref, q_ref, k_ref, v_ref, o_ref, m_scr, acc_scr, *, tq):
    t = pl.program_id(1)
    qi = qi_ref[t]
    kj = kj_ref[t]

    @pl.when(kj == 0)
    def _():
        m_scr[...] = jnp.full(m_scr.shape, NEG_INF, F32)
        acc_scr[...] = jnp.zeros(acc_scr.shape, F32)

    def step(masked):
        if masked:
            row = lax.broadcasted_iota(jnp.int32, (tq, tq), 0) // CHUNK
            col = lax.broadcasted_iota(jnp.int32, (tq, tq), 1) // CHUNK
            visible = col <= row
        for hd in range(MLA_HEADS):
            s = _dot_nt(q_ref[hd], k_ref[hd])
            if masked:
                s = jnp.where(visible, s, NEG_INF)
            m_prev = m_scr[hd]
            m_new = jnp.maximum(m_prev, jnp.max(s, axis=1, keepdims=True))
            p = jnp.exp2(s - jnp.tile(m_new, (1, tq // LANES)))
            m_scr[hd] = m_new
            acc_scr[hd] = acc_scr[hd] * jnp.exp2(m_prev - m_new) + _dot(p.astype(BF16), v_ref[hd])

    @pl.when(kj < qi)
    def _():
        step(False)

    @pl.when(kj == qi)
    def _():
        step(True)
        lane = lax.broadcasted_iota(jnp.int32, (tq, LANES), 1)
        for p in range(MLA_HEADS // 2):
            even, odd = acc_scr[2 * p], acc_scr[2 * p + 1]
            even = even / even[:, MLA_V:MLA_V + 1]
            odd = odd / odd[:, MLA_V:MLA_V + 1]
            pair = jnp.where(lane < MLA_V, even, pltpu.roll(odd, MLA_V, axis=1))
            o_ref[:, p * LANES:(p + 1) * LANES] = pair.astype(BF16)


def _mla_attention(q, k, v):
    B, H, S, _ = q.shape
    tq = TQ_MLA
    n = S // tq
    pairs = [(i, j) for i in range(n) for j in range(i + 1)]
    qi = jnp.asarray(np.array([p[0] for p in pairs], np.int32))
    kj = jnp.asarray(np.array([p[1] for p in pairs], np.int32))
    grid_spec = pltpu.PrefetchScalarGridSpec(
        num_scalar_prefetch=2,
        grid=(B, len(pairs)),
        in_specs=[pl.BlockSpec((None, H, tq, LANES), lambda b, t, qi, kj: (b, 0, qi[t], 0)),
                  pl.BlockSpec((None, H, tq, LANES), lambda b, t, qi, kj: (b, 0, kj[t], 0)),
                  pl.BlockSpec((None, H, tq, LANES), lambda b, t, qi, kj: (b, 0, kj[t], 0))],
        out_specs=pl.BlockSpec((None, tq, H * MLA_V), lambda b, t, qi, kj: (b, qi[t], 0)),
        scratch_shapes=[pltpu.VMEM((H, tq, LANES), F32)] * 2,
    )
    return pl.pallas_call(
        functools.partial(_mla_kernel, tq=tq),
        grid_spec=grid_spec,
        out_shape=jax.ShapeDtypeStruct((B, S, H * MLA_V), BF16),
        compiler_params=_params("parallel", "arbitrary"),
        name="mla_attention",
    )(qi, kj, q, k, v)


def _rel_bias_kernel(tab_ref, o_ref):
    r = lax.broadcasted_iota(jnp.int32, (CHUNK, BAND * CHUNK), 0)
    c = lax.broadcasted_iota(jnp.int32, (CHUNK, BAND * CHUNK), 1)
    rel = jnp.clip(LEFT_CHUNKS * CHUNK + r - c, -REL_CLIP, REL_CLIP) + REL_CLIP
    t_lo = REL_CLIP + max(-REL_CLIP, LEFT_CHUNKS * CHUNK - (BAND * CHUNK - 1))
    t_hi = REL_CLIP + min(REL_CLIP, LEFT_CHUNKS * CHUNK + CHUNK - 1)
    for hd in range(CA_HEADS):
        def body(t, acc):
            return jnp.where(rel == t, tab_ref[hd, t], acc)
        o_ref[hd] = lax.fori_loop(t_lo, t_hi + 1, body, jnp.zeros(rel.shape, F32))


def _rel_bias_band(rel_bias):
    depth = rel_bias.shape[0]
    return pl.pallas_call(
        _rel_bias_kernel,
        grid=(depth,),
        in_specs=[pl.BlockSpec((None, CA_HEADS, 2 * REL_CLIP + 1), lambda d: (d, 0, 0),
                               memory_space=pltpu.SMEM)],
        out_specs=pl.BlockSpec((None, CA_HEADS, CHUNK, BAND * CHUNK), lambda d: (d, 0, 0, 0)),
        out_shape=jax.ShapeDtypeStruct((depth, CA_HEADS, CHUNK, BAND * CHUNK), F32),
        compiler_params=_params("parallel"),
        name="rel_bias_band",
    )(rel_bias)


def _bias_tiles(band):
    depth = band.shape[0]
    n_qc = TQ_CA // CHUNK
    n_kc = CA_WIN // CHUNK
    neg = jnp.full((depth, CA_HEADS, CHUNK, CHUNK), NEG_INF, F32)
    rows = []
    for rc in range(n_qc):
        blocks = []
        for cc in range(n_kc):
            j = cc - rc
            blocks.append(band[..., j * CHUNK:(j + 1) * CHUNK] if 0 <= j < BAND else neg)
        rows.append(jnp.concatenate(blocks, axis=-1))
    tile = jnp.concatenate(rows, axis=-2) * np.log2(np.e)
    col = jnp.arange(CA_WIN)
    variants = [jnp.where(col >= LEFT_CHUNKS * CHUNK - i * TQ_CA, tile, NEG_INF)
                for i in range(N_CA_LEAD)] + [tile]
    return jnp.stack(variants, axis=1)


def _chunk_attn_kernel(q_ref, k2_ref, k1_ref, k0_ref, v2_ref, v1_ref, v0_ref, bias_ref, o_ref):
    tq = TQ_CA
    lane_q = lax.broadcasted_iota(jnp.int32, (tq, LANES), 1)
    for p in range(CA_HEADS // 2):
        sl = slice(p * LANES, (p + 1) * LANES)
        q = q_ref[:, sl]
        k = jnp.concatenate([k2_ref[:, sl], k1_ref[:, sl], k0_ref[:, sl]], axis=0)
        v = jnp.concatenate([v2_ref[:, sl], v1_ref[:, sl], v0_ref[:, sl]], axis=0)
        outs = []
        for half in range(2):
            keep = (lane_q < CA_HEAD_DIM) if half == 0 else (lane_q >= CA_HEAD_DIM)
            s = _dot_nt(jnp.where(keep, q, jnp.zeros_like(q)), k) + bias_ref[2 * p + half]
            m = jnp.max(s, axis=1, keepdims=True)
            e = jnp.exp2(s - m)
            l = jnp.sum(e, axis=1, keepdims=True)
            outs.append(_dot(e.astype(BF16), v) / l)
        o_ref[:, sl] = jnp.where(lane_q < CA_HEAD_DIM, outs[0], outs[1]).astype(BF16)


def _chunk_attention(qb, kb, vb, bias):
    B, S, _ = qb.shape
    tq = TQ_CA
    n = S // tq
    cur = pl.BlockSpec((None, tq, CA_DIM), lambda b, i: (b, i, 0))
    prev1 = pl.BlockSpec((None, tq, CA_DIM), lambda b, i: (b, jnp.maximum(i - 1, 0), 0))
    prev2 = pl.BlockSpec((None, tq, CA_DIM), lambda b, i: (b, jnp.maximum(i - 2, 0), 0))
    return pl.pallas_call(
        _chunk_attn_kernel,
        grid=(B, n),
        in_specs=[cur, prev2, prev1, cur, prev2, prev1, cur,
                  pl.BlockSpec((None,) + bias.shape[1:],
                               lambda b, i: (jnp.minimum(i, N_CA_LEAD), 0, 0, 0))],
        out_specs=cur,
        out_shape=jax.ShapeDtypeStruct((B, S, CA_DIM), BF16),
        compiler_params=_params("parallel", "parallel"),
        name="chunk_attention",
    )(qb, kb, kb, kb, vb, vb, vb, bias)


def _post_attn_kernel(x_ref, a_ref, b_ref, woa_ref, wob_ref, fn_ref, *rest, route):
    if route:
        wr_ref, x2_out, h_out, meta_out, gate_out, cnt_out, run_scr = rest
    else:
        x2_out, h_out = rest
    x2 = x_ref[...] + _dot(a_ref[...], woa_ref[...]) + _dot(b_ref[...], wob_ref[...])
    x2_out[...] = x2
    h = _rms(x2, fn_ref[...])
    h_out[...] = h.astype(h_out.dtype)
    if not route:
        return

    tm = x2.shape[0]
    step = pl.program_id(0)

    @pl.when(step == 0)
    def _():
        run_scr[...] = jnp.zeros(run_scr.shape, F32)

    lane = lax.broadcasted_iota(jnp.int32, (tm, LANES), 1)
    h_hi = h.astype(BF16)
    h_lo = (h - h_hi.astype(F32)).astype(BF16)
    logits = _dot(h_hi, wr_ref[0]) + (_dot(h_lo, wr_ref[0]) + _dot(h_hi, wr_ref[1]))
    logits = jnp.where(lane < N_EXPERTS, logits, -jnp.inf)
    v1 = jnp.max(logits, axis=1, keepdims=True)
    e1 = jnp.min(jnp.where(logits == v1, lane, LANES), axis=1, keepdims=True)
    rest_logits = jnp.where(lane == e1, -jnp.inf, logits)
    v2 = jnp.max(rest_logits, axis=1, keepdims=True)
    e2 = jnp.min(jnp.where(rest_logits == v2, lane, LANES), axis=1, keepdims=True)
    w2 = jnp.exp(v2 - v1)
    g1 = 1.0 / (1.0 + w2)
    g2 = w2 / (1.0 + w2)

    sel1 = lane == e1
    sel2 = lane == e2
    cnt = jnp.where(sel1 | sel2, 1.0, 0.0)
    r_i = lax.broadcasted_iota(jnp.int32, (tm, tm), 0)
    c_i = lax.broadcasted_iota(jnp.int32, (tm, tm), 1)
    lower = jnp.where(c_i < r_i, 1.0, 0.0).astype(BF16)
    before = _dot(lower, cnt.astype(BF16)) + run_scr[...]
    rank1 = jnp.sum(jnp.where(sel1, before, 0.0), axis=1, keepdims=True)
    rank2 = jnp.sum(jnp.where(sel2, before, 0.0), axis=1, keepdims=True)
    run_scr[...] = run_scr[...] + jnp.sum(cnt, axis=0, keepdims=True)
    cnt_out[...] = run_scr[...]

    meta = jnp.where(lane == 0, e1, jnp.where(lane == 1, e2, jnp.where(
        lane == 2, rank1.astype(jnp.int32), jnp.where(lane == 3, rank2.astype(jnp.int32), 0))))
    meta_out[...] = meta
    gate_out[...] = jnp.where(lane == 0, g1, jnp.where(lane == 1, g2, 0.0))


def _post_attn(x, a, b, lw, route):
    T, D = x.shape
    tm = TM_PROJ
    half = a.shape[1]
    tok = lambda w: pl.BlockSpec((tm, w), lambda i: (i, 0))
    full = lambda arr: pl.BlockSpec(arr.shape, lambda i: (0,) * arr.ndim)
    ins = [x, a, b, lw['w_out_a'], lw['w_out_b'], lw['ffn_norm']]
    in_specs = [tok(D), tok(half), tok(half), full(ins[3]), full(ins[4]), full(ins[5])]
    out_specs = [tok(D), tok(D)]
    out_shape = [jax.ShapeDtypeStruct((T, D), F32),
                 jax.ShapeDtypeStruct((T, D), F32 if route else BF16)]
    scratch = []
    if route:
        ins.append(lw['w_router'])
        in_specs.append(full(lw['w_router']))
        out_specs += [tok(LANES), tok(LANES), pl.BlockSpec((1, LANES), lambda i: (0, 0))]
        out_shape += [jax.ShapeDtypeStruct((T, LANES), jnp.int32),
                      jax.ShapeDtypeStruct((T, LANES), F32),
                      jax.ShapeDtypeStruct((1, LANES), F32)]
        scratch = [pltpu.VMEM((1, LANES), F32)]
    return pl.pallas_call(
        functools.partial(_post_attn_kernel, route=route),
        grid=(T // tm,),
        in_specs=in_specs,
        out_specs=out_specs,
        out_shape=out_shape,
        scratch_shapes=scratch,
        compiler_params=_params("arbitrary"),
        name="post_attn_route" if route else "post_attn",
    )(*ins)


def _ffn_kernel(be_ref, nv_ref, h_ref, *rest, residual):
    if residual:
        res_ref, w1_ref, w3_ref, w2_ref, o_ref = rest
    else:
        w1_ref, w3_ref, w2_ref, o_ref = rest
    i = pl.program_id(0)
    f = pl.program_id(1)

    @pl.when(f == 0)
    def _():
        o_ref[...] = res_ref[...] if residual else jnp.zeros(o_ref.shape, F32)

    @pl.when(i < nv_ref[0])
    def _():
        h = h_ref[...].astype(BF16)
        a = _dot(h, w1_ref[...])
        g = _dot(h, w3_ref[...])
        act = (a * jax.nn.sigmoid(a) * g).astype(BF16)
        o_ref[...] += _dot(act, w2_ref[...])


def _ffn(h, res, w1, w3, w2, block_expert, n_valid):
    R, D = h.shape
    tm, tf = TM_FFN, TF_FFN
    n_f = FFN_DIM // tf
    last = lambda i, nv: jnp.minimum(i, nv[0] - 1)
    row = pl.BlockSpec((tm, D), lambda i, f, be, nv: (last(i, nv), 0))
    out = pl.BlockSpec((tm, D), lambda i, f, be, nv: (i, 0))
    fcl = lambda i, f, nv: jnp.where(i < nv[0], f, n_f - 1)
    w13 = pl.BlockSpec((None, D, tf), lambda i, f, be, nv: (be[last(i, nv)], 0, fcl(i, f, nv)))
    w2s = pl.BlockSpec((None, tf, D), lambda i, f, be, nv: (be[last(i, nv)], fcl(i, f, nv), 0))
    residual = res is not None
    ins = [h] + ([res] if residual else []) + [w1, w3, w2]
    in_specs = [row] + ([out] if residual else []) + [w13, w13, w2s]
    return pl.pallas_call(
        functools.partial(_ffn_kernel, residual=residual),
        grid_spec=pltpu.PrefetchScalarGridSpec(
            num_scalar_prefetch=2, grid=(R // tm, n_f), in_specs=in_specs, out_specs=out),
        out_shape=jax.ShapeDtypeStruct((R, D), F32),
        compiler_params=_params("parallel", "arbitrary"),
        name="swiglu_res" if residual else "swiglu_expert",
    )(block_expert, n_valid, *ins)


def _start_rows(copy, n):
    def body(r, c):
        for k in range(TOP_K):
            copy(r, k).start(priority=k)
        return c
    lax.fori_loop(0, n, body, 0, unroll=MOVE_UNROLL)


def _wait_rows(copy, n):
    def body(r, c):
        for k in range(TOP_K):
            copy(r, k).wait()
        return c
    lax.fori_loop(0, n, body, 0, unroll=MOVE_UNROLL)


def _dispatch_kernel(dest_ref, h_ref, xs_in, xs_out, sem):
    del xs_in

    def copy(r, k):
        return pltpu.make_async_copy(h_ref.at[pl.ds(r, 1)],
                                     xs_out.at[pl.ds(dest_ref[TOP_K * r + k], 1)], sem)

    _start_rows(copy, TM_DISPATCH)
    _wait_rows(copy, TM_DISPATCH)


def _dispatch(h, dest_flat, n_rows):
    T, D = h.shape
    tm = TM_DISPATCH
    zeros = jnp.zeros((n_rows, D), h.dtype)
    return pl.pallas_call(
        _dispatch_kernel,
        grid=(T // tm,),
        in_specs=[pl.BlockSpec((TOP_K * tm,), lambda i: (i,), memory_space=pltpu.SMEM),
                  pl.BlockSpec((tm, D), lambda i: (i, 0)),
                  pl.BlockSpec(memory_space=pl.ANY)],
        out_specs=pl.BlockSpec(memory_space=pl.ANY),
        out_shape=jax.ShapeDtypeStruct((n_rows, D), h.dtype),
        scratch_shapes=[pltpu.SemaphoreType.DMA(())],
        input_output_aliases={2: 0},
        compiler_params=_params("arbitrary"),
        name="moe_dispatch",
    )(dest_flat, h, zeros)


def _combine_kernel(dest_ref, next_ref, x_ref, gate_ref, g_ref, ys_hbm, o_ref, buf, sems, *, final):
    tm = TM_MOVE
    i = pl.program_id(0)
    slot = i % 2

    def copies(idx_ref, s):
        def copy(r, k):
            return pltpu.make_async_copy(ys_hbm.at[pl.ds(idx_ref[TOP_K * r + k], 1)],
                                         buf.at[s, k, pl.ds(r, 1)], sems.at[s])
        return copy

    @pl.when(i == 0)
    def _():
        _start_rows(copies(dest_ref, slot), tm)

    @pl.when(i + 1 < pl.num_programs(0))
    def _():
        _start_rows(copies(next_ref, 1 - slot), tm)

    _wait_rows(copies(dest_ref, slot), tm)
    gates = gate_ref[...]
    y = x_ref[...] + (buf[slot, 0] * gates[:, 0:1] + buf[slot, 1] * gates[:, 1:2])
    o_ref[...] = _rms(y, g_ref[...]) if final else y


def _combine(x, ys, gates, dest_flat, final_g):
    T, D = x.shape
    tm = TM_MOVE
    n = T // tm
    final = final_g is not None
    g = final_g if final else jnp.ones((1, D), F32)
    return pl.pallas_call(
        functools.partial(_combine_kernel, final=final),
        grid=(n,),
        in_specs=[pl.BlockSpec((TOP_K * tm,), lambda i: (i,), memory_space=pltpu.SMEM),
                  pl.BlockSpec((TOP_K * tm,), lambda i: (jnp.minimum(i + 1, n - 1),),
                               memory_space=pltpu.SMEM),
                  pl.BlockSpec((tm, D), lambda i: (i, 0)),
                  pl.BlockSpec((tm, LANES), lambda i: (i, 0)),
                  pl.BlockSpec((1, D), lambda i: (0, 0)),
                  pl.BlockSpec(memory_space=pl.ANY)],
        out_specs=pl.BlockSpec((tm, D), lambda i: (i, 0)),
        out_shape=jax.ShapeDtypeStruct((T, D), F32),
        scratch_shapes=[pltpu.VMEM((2, TOP_K, tm, D), F32), pltpu.SemaphoreType.DMA((2,))],
        compiler_params=_params("arbitrary"),
        name="moe_combine_final" if final else "moe_combine",
    )(dest_flat, dest_flat, x, gates, g, ys)


def _final_norm_kernel(x_ref, g_ref, o_ref):
    o_ref[...] = _rms(x_ref[...], g_ref[...])


def _final_norm(x, g):
    T, D = x.shape
    tm = TM_PROJ
    return pl.pallas_call(
        _final_norm_kernel,
        grid=(T // tm,),
        in_specs=[pl.BlockSpec((tm, D), lambda i: (i, 0)), pl.BlockSpec((1, D), lambda i: (0, 0))],
        out_specs=pl.BlockSpec((tm, D), lambda i: (i, 0)),
        out_shape=jax.ShapeDtypeStruct((T, D), F32),
        compiler_params=_params("parallel"),
        name="final_norm",
    )(x, g)


def _rope_tables(seq):
    inv = 1.0 / (ROPE_THETA ** (jnp.arange(0, MLA_ROPE, 2, dtype=F32) / MLA_ROPE))
    ang = jnp.arange(seq, dtype=F32)[:, None] * inv[None, :]
    cos, sin = jnp.cos(ang), jnp.sin(ang)
    ones = jnp.ones((seq, MLA_NOPE), F32)
    z_nope = jnp.zeros((seq, MLA_NOPE), F32)
    z_pad = jnp.zeros((seq, LANES - MLA_NOPE - MLA_ROPE), F32)
    scale = (MLA_NOPE + MLA_ROPE) ** -0.5 * np.log2(np.e)
    cq =jnp.concatenate([ones, cos, cos, z_pad], axis=1) * scale
    sq = jnp.concatenate([z_nope, sin, sin, z_pad], axis=1) * scale
    ck = jnp.concatenate([z_nope, cos, cos, z_pad], axis=1)
    sk = jnp.concatenate([z_nope, sin, sin, z_pad], axis=1)
    return cq, sq, ck, sk


def _layer_weights(i, attn_norm, w_in, q_norm, w_uq, kv_norm, w_ukv, w_out, ffn_norm):
    w = w_in[i]
    o = np.cumsum([0, Q_LORA, KV_LORA, MLA_ROPE])
    w_cq, w_ckv, w_kr = (w[:, o[j]:o[j + 1]] for j in range(3))
    w_qkvb = w[:, o[3]:]
    d = w.shape[0]
    x1, x2 = w_kr[:, :HALF_ROPE], w_kr[:, HALF_ROPE:]
    z_nope = jnp.zeros((d, MLA_NOPE), F32)
    z_pad = jnp.zeros((d, LANES - MLA_NOPE - MLA_ROPE), F32)
    w_kr2 = jnp.concatenate([z_nope, x1, x2, z_pad, z_nope, -x2, x1, z_pad], axis=1)

    wq = w_uq[i].reshape(Q_LORA, MLA_HEADS, MLA_NOPE + MLA_ROPE)
    nope, q1, q2 = wq[..., :MLA_NOPE], wq[..., MLA_NOPE:MLA_NOPE + HALF_ROPE], wq[..., MLA_NOPE + HALF_ROPE:]
    zq_nope = jnp.zeros((Q_LORA, MLA_HEADS, MLA_NOPE), F32)
    zq_pad = jnp.zeros((Q_LORA, MLA_HEADS, LANES - MLA_NOPE - MLA_ROPE), F32)
    pre = jnp.concatenate([nope, q1, q2, zq_pad], axis=-1).reshape(Q_LORA, MLA_HEADS * LANES)
    swap = jnp.concatenate([zq_nope, -q2, q1, zq_pad], axis=-1).reshape(Q_LORA, MLA_HEADS * LANES)

    wkv = w_ukv[i].reshape(KV_LORA, MLA_HEADS, MLA_NOPE + MLA_V)
    zk = jnp.zeros((KV_LORA, MLA_HEADS, LANES - MLA_NOPE), F32)
    w_uk = jnp.concatenate([wkv[..., :MLA_NOPE], zk], axis=-1).reshape(KV_LORA, MLA_HEADS * LANES)
    zv = jnp.zeros((KV_LORA, MLA_HEADS, LANES - MLA_V), F32)
    w_uv = jnp.concatenate([wkv[..., MLA_NOPE:], zv], axis=-1).reshape(KV_LORA, MLA_HEADS * LANES)
    half = MLA_HEADS * MLA_V
    return {
        'attn_norm': attn_norm[i][None], 'w_cq': w_cq.astype(BF16), 'w_ckv': w_ckv.astype(BF16),
        'w_kr': w_kr2.astype(BF16), 'w_qkvb': w_qkvb.astype(BF16), 'q_norm': q_norm[i][None],
        'w_uq': jnp.concatenate([pre, swap], axis=1).astype(BF16), 'kv_norm': kv_norm[i][None],
        'w_uk': w_uk.astype(BF16), 'w_uv': w_uv.astype(BF16),
        'w_out_a': w_out[i][:half].astype(BF16), 'w_out_b': w_out[i][half:].astype(BF16),
        'ffn_norm': ffn_norm[i][None],
    }


def _moe_layout(meta, counts):
    T = meta.shape[0]
    blk = TM_FFN
    cnt = counts[0, :N_EXPERTS].astype(jnp.int32)
    padded = (cnt + blk - 1) // blk * blk
    pends = jnp.cumsum(padded)
    pstarts = pends - padded
    dest = pstarts[meta[:, :TOP_K]] + meta[:, TOP_K:2 * TOP_K]
    n_rows = T * TOP_K + N_EXPERTS * blk
    n_blocks = n_rows // blk
    block_expert = jnp.minimum(
        jnp.searchsorted(pends, jnp.arange(n_blocks, dtype=jnp.int32) * blk, side='right'),
        N_EXPERTS - 1).astype(jnp.int32)
    n_valid = (pends[-1:] // blk).astype(jnp.int32)
    return dest.reshape(-1).astype(jnp.int32), block_expert, n_valid, n_rows


def kernel(x, attn_norm, w_in, q_norm, w_uq, kv_norm, w_ukv, rel_bias, w_out, ffn_norm, dense_w1,
           dense_w3, dense_w2, w_router, moe_w1, moe_w3, moe_w2, final_norm):
    B, S, D = x.shape
    T = B * S
    depth = w_in.shape[0]
    tabs = _rope_tables(S)
    bias = _bias_tiles(_rel_bias_band(rel_bias))
    dense_be = jnp.zeros((T // TM_FFN,), jnp.int32)
    dense_nv = jnp.full((1,), T // TM_FFN, jnp.int32)
    out = None
    for i in range(depth):
        lw = _layer_weights(i, attn_norm, w_in, q_norm, w_uq, kv_norm, w_ukv, w_out, ffn_norm)
        q, k, v, qb, kb, vb = _pre_attn(x, lw, tabs)
        a = _mla_attention(q, k, v)
        b = _chunk_attention(qb, kb, vb, bias[i])
        xt = x.reshape(T, D)
        a2, b2 = a.reshape(T, -1), b.reshape(T, -1)
        j = i // 2
        last = i == depth - 1
        if i % 2 == 0:
            x2, h = _post_attn(xt, a2, b2, lw, route=False)
            y = _ffn(h, x2, dense_w1[j][None].astype(BF16), dense_w3[j][None].astype(BF16),
                     dense_w2[j][None].astype(BF16), dense_be, dense_nv)
            if last:
                out = _final_norm(y, final_norm[None])
        else:
            router = jnp.zeros((D, LANES), F32).at[:, :N_EXPERTS].set(w_router[j])
            router_hi = router.astype(BF16)
            router_lo = (router - router_hi.astype(F32)).astype(BF16)
            lw['w_router'] = jnp.stack([router_hi, router_lo])
            x2, h, meta, gates, counts = _post_attn(xt, a2, b2, lw, route=True)
            dest, block_expert, n_valid, n_rows = _moe_layout(meta, counts)
            xs = _dispatch(h, dest, n_rows)
            ys = _ffn(xs, None, moe_w1[j].astype(BF16), moe_w3[j].astype(BF16),
                      moe_w2[j].astype(BF16), block_expert, n_valid)
            y = _combine(x2, ys, gates, dest, final_norm[None] if last else None)
            if last:
                out = y
        x = y.reshape(B, S, D)
    return out.reshape(B, S, D)
```

```python
import functools

import jax
import jax.numpy as jnp
import numpy as np
from jax import lax
from jax.experimental import pallas as pl
from jax.experimental.pallas import tpu as pltpu

D_MODEL = 1024
CHUNK = 64
MLA_HEADS = 8
MLA_NOPE = 64
MLA_ROPE = 32
MLA_V = 64
Q_LORA = 384
KV_LORA = 256
ROPE_THETA = 10000.0
CA_HEADS = 8
CA_HEAD_DIM = 64
LEFT_CHUNKS = 8
REL_CLIP = 128
FFN_DIM = 3584
N_EXPERTS = 8
TOP_K = 2
EPS = 1e-6
NEG_INF = -1e30

CA_DIM = CA_HEADS * CA_HEAD_DIM
BAND = LEFT_CHUNKS + 1
HALF_ROPE = MLA_ROPE // 2
LANES = 128
VMEM_LIMIT = 56 * 1024 * 1024

TM_PROJ = 512
TQ_MLA = 1024
TK_MLA = 512
TQ_CA = 256
CA_WIN = LEFT_CHUNKS * CHUNK + TQ_CA
N_CA_LEAD = LEFT_CHUNKS * CHUNK // TQ_CA
TM_FFN = 512
TF_FFN = 1792
TM_DISPATCH = 512
TM_MOVE = 256
MOVE_UNROLL = 8

BF16 = jnp.bfloat16
F32 = jnp.float32


def _params(*sem):
    return pltpu.CompilerParams(dimension_semantics=sem, vmem_limit_bytes=VMEM_LIMIT)


def _rms(x, g):
    ms = jnp.mean(x * x, axis=-1, keepdims=True)
    return x * lax.rsqrt(ms + EPS) * g


def _dot(a, b):
    return jnp.dot(a, b, preferred_element_type=F32)


def _dot_nt(a, b):
    return lax.dot_general(a, b, (((1,), (1,)), ((), ())), preferred_element_type=F32)


def _pre_attn_kernel(x_ref, an_ref, wcq_ref, wckv_ref, wkr_ref, wqkvb_ref, qn_ref, wuq_ref,
                     kvn_ref, wuk_ref, wuv_ref, cq_ref, sq_ref, ck_ref, sk_ref,
                     q_out, k_out, v_out, qb_out, kb_out, vb_out):
    h = _rms(x_ref[...], an_ref[...]).astype(BF16)
    c_q = _dot(h, wcq_ref[...])
    c_kv = _dot(h, wckv_ref[...])
    kr2 = _dot(h, wkr_ref[...])
    qkvb = _dot(h, wqkvb_ref[...])
    qb_out[...] = (qkvb[:, :CA_DIM] * (CA_HEAD_DIM ** -0.5 * np.log2(np.e))).astype(BF16)
    kb_out[...] = qkvb[:, CA_DIM:2 * CA_DIM].astype(BF16)
    vb_out[...] = qkvb[:, 2 * CA_DIM:].astype(BF16)

    q2 = _dot(_rms(c_q, qn_ref[...]).astype(BF16), wuq_ref[...])
    cq, sq = cq_ref[...], sq_ref[...]
    width = MLA_HEADS * LANES
    for hd in range(MLA_HEADS):
        lo = hd * LANES
        q_out[hd] = (q2[:, lo:lo + LANES] * cq + q2[:, width + lo:width + lo + LANES] * sq).astype(BF16)

    ckvn = _rms(c_kv, kvn_ref[...]).astype(BF16)
    kn = _dot(ckvn, wuk_ref[...])
    vv = _dot(ckvn, wuv_ref[...])
    kr = kr2[:, :LANES] * ck_ref[...] + kr2[:, LANES:] * sk_ref[...]
    lane = lax.broadcasted_iota(jnp.int32, kr.shape, 1)
    one = jnp.where(lane == MLA_V, 1.0, 0.0)
    for hd in range(MLA_HEADS):
        k_out[hd] = (kn[:, hd * LANES:(hd + 1) * LANES] + kr).astype(BF16)
        v_out[hd] = (vv[:, hd * LANES:(hd + 1) * LANES] + one).astype(BF16)


def _pre_attn(x, lw, tabs):
    B, S, D = x.shape
    tm = TM_PROJ
    n_s = S // tm
    full = lambda a: pl.BlockSpec(a.shape, lambda b, i: (0,) * a.ndim)
    tab = pl.BlockSpec((tm, LANES), lambda b, i: (i, 0))
    tok = lambda w: pl.BlockSpec((None, tm, w), lambda b, i: (b, i, 0))
    heads = lambda n: pl.BlockSpec((None, n, tm, LANES), lambda b, i: (b, 0, i, 0))
    weights = [lw['attn_norm'], lw['w_cq'], lw['w_ckv'], lw['w_kr'], lw['w_qkvb'], lw['q_norm'],
               lw['w_uq'], lw['kv_norm'], lw['w_uk'], lw['w_uv']]
    return pl.pallas_call(
        _pre_attn_kernel,
        grid=(B, n_s),
        in_specs=[tok(D)] + [full(w) for w in weights] + [tab] * 4,
        out_specs=[heads(MLA_HEADS), heads(MLA_HEADS), heads(MLA_HEADS),
                   tok(CA_DIM), tok(CA_DIM), tok(CA_DIM)],
        out_shape=[jax.ShapeDtypeStruct((B, MLA_HEADS, S, LANES), BF16),
                   jax.ShapeDtypeStruct((B, MLA_HEADS, S, LANES), BF16),
                   jax.ShapeDtypeStruct((B, MLA_HEADS, S, LANES), BF16),
                   jax.ShapeDtypeStruct((B, S, CA_DIM), BF16),
                   jax.ShapeDtypeStruct((B, S, CA_DIM), BF16),
                   jax.ShapeDtypeStruct((B, S, CA_DIM), BF16)],
        compiler_params=_params("parallel", "parallel"),
        name="pre_attn",
    )(x, *weights, *tabs)


def _mla_kernel(qi_ref, kj_ref, q_ref, k_ref, v_ref, o_ref, m_scr, acc_scr, *, tq, tk):
    t = pl.program_id(1)
    kj = kj_ref[t]
    ratio = tq // tk
    diag = kj - qi_ref[t] * ratio

    @pl.when(kj == 0)
    def _():
        m_scr[...] = jnp.full(m_scr.shape, NEG_INF, F32)
        acc_scr[...] = jnp.zeros(acc_scr.shape, F32)

    def update(hd, lo, n, visible):
        s = _dot_nt(q_ref[hd, lo:lo + n], k_ref[hd])
        if visible is not None:
            s = jnp.where(visible, s, NEG_INF)
        m_prev = m_scr[hd, lo:lo + n]
        m_new = jnp.maximum(m_prev, jnp.max(s, axis=1, keepdims=True))
        p = jnp.exp2((s - jnp.tile(m_new, (1, tk // LANES))).astype(BF16))
        m_scr[hd, lo:lo + n] = m_new
        acc_scr[hd, lo:lo + n] = (acc_scr[hd, lo:lo + n] * jnp.exp2(m_prev - m_new)
                                  + _dot(p, v_ref[hd]))

    @pl.when(diag < 0)
    def _():
        for hd in range(MLA_HEADS):
            update(hd, 0, tq, None)

    for d in range(ratio):
        @pl.when(diag == d)
        def _():
            row = lax.broadcasted_iota(jnp.int32, (tk, tk), 0) // CHUNK
            col = lax.broadcasted_iota(jnp.int32, (tk, tk), 1) // CHUNK
            visible = col <= row
            below = (d + 1) * tk
            for hd in range(MLA_HEADS):
                update(hd, d * tk, tk, visible)
                if below < tq:
                    update(hd, below, tq - below, None)

    @pl.when(diag == ratio - 1)
    def _():
        lane = lax.broadcasted_iota(jnp.int32, (tq, LANES), 1)
        for p in range(MLA_HEADS // 2):
            even, odd = acc_scr[2 * p], acc_scr[2 * p + 1]
            even = even / even[:, MLA_V:MLA_V + 1]
            odd = odd / odd[:, MLA_V:MLA_V + 1]
            pair = jnp.where(lane < MLA_V, even, pltpu.roll(odd, MLA_V, axis=1))
            o_ref[:, p * LANES:(p + 1) * LANES] = pair.astype(BF16)


def _mla_attention(q, k, v):
    B, H, S, _ = q.shape
    tq, tk = TQ_MLA, TK_MLA
    ratio = tq // tk
    pairs = [(i, j) for i in range(S // tq) for j in range(ratio * (i + 1))]
    qi = jnp.asarray(np.array([p[0] for p in pairs], np.int32))
    kj = jnp.asarray(np.array([p[1] for p in pairs], np.int32))
    kv_spec = pl.BlockSpec((None, H, tk, LANES), lambda b, t, qi, kj: (b, 0, kj[t], 0))
    grid_spec = pltpu.PrefetchScalarGridSpec(
        num_scalar_prefetch=2,
        grid=(B, len(pairs)),
        in_specs=[pl.BlockSpec((None, H, tq, LANES), lambda b, t, qi, kj: (b, 0, qi[t], 0)),
                  kv_spec, kv_spec],
        out_specs=pl.BlockSpec((None, tq, H * MLA_V), lambda b, t, qi, kj: (b, qi[t], 0)),
        scratch_shapes=[pltpu.VMEM((H, tq, LANES), F32)] * 2,
    )
    return pl.pallas_call(
        functools.partial(_mla_kernel, tq=tq, tk=tk),
        grid_spec=grid_spec,
        out_shape=jax.ShapeDtypeStruct((B, S, H * MLA_V), BF16),
        compiler_params=_params("parallel", "arbitrary"),
        name="mla_attention",
    )(qi, kj, q, k, v)


def _rel_bias_kernel(tab_ref, o_ref):
    r = lax.broadcasted_iota(jnp.int32, (CHUNK, BAND * CHUNK), 0)
    c = lax.broadcasted_iota(jnp.int32, (CHUNK, BAND * CHUNK), 1)
    rel = jnp.clip(LEFT_CHUNKS * CHUNK + r - c, -REL_CLIP, REL_CLIP) + REL_CLIP
    t_lo = REL_CLIP + max(-REL_CLIP, LEFT_CHUNKS * CHUNK - (BAND * CHUNK - 1))
    t_hi = REL_CLIP + min(REL_CLIP, LEFT_CHUNKS * CHUNK + CHUNK - 1)
    for hd in range(CA_HEADS):
        def body(t, acc):
            return jnp.where(rel == t, tab_ref[hd, t], acc)
        o_ref[hd] = lax.fori_loop(t_lo, t_hi + 1, body, jnp.zeros(rel.shape, F32))


def _rel_bias_band(rel_bias):
    depth = rel_bias.shape[0]
    return pl.pallas_call(
        _rel_bias_kernel,
        grid=(depth,),
        in_specs=[pl.BlockSpec((None, CA_HEADS, 2 * REL_CLIP + 1), lambda d: (d, 0, 0),
                               memory_space=pltpu.SMEM)],
        out_specs=pl.BlockSpec((None, CA_HEADS, CHUNK, BAND * CHUNK), lambda d: (d, 0, 0, 0)),
        out_shape=jax.ShapeDtypeStruct((depth, CA_HEADS, CHUNK, BAND * CHUNK), F32),
        compiler_params=_params("parallel"),
        name="rel_bias_band",
    )(rel_bias)


def _bias_tiles(band):
    depth = band.shape[0]
    n_qc = TQ_CA // CHUNK
    n_kc = CA_WIN // CHUNK
    neg = jnp.full((depth, CA_HEADS, CHUNK, CHUNK), NEG_INF, F32)
    rows = []
    for rc in range(n_qc):
        blocks = []
        for cc in range(n_kc):
            j = cc - rc
            blocks.append(band[..., j * CHUNK:(j + 1) * CHUNK] if 0 <= j < BAND else neg)
        rows.append(jnp.concatenate(blocks, axis=-1))
    tile = jnp.concatenate(rows, axis=-2) * np.log2(np.e)
    col = jnp.arange(CA_WIN)
    variants = [jnp.where(col >= LEFT_CHUNKS * CHUNK - i * TQ_CA, tile, NEG_INF)
                for i in range(N_CA_LEAD)] + [tile]
    return jnp.stack(variants, axis=1)


def _chunk_attn_kernel(q_ref, k2_ref, k1_ref, k0_ref, v2_ref, v1_ref, v0_ref, bias_ref, o_ref):
    tq, win = TQ_CA, CA_WIN
    lane_q = lax.broadcasted_iota(jnp.int32, (tq, LANES), 1)
    first = lane_q < CA_HEAD_DIM
    for p in range(CA_HEADS // 2):
        sl = slice(p * LANES, (p + 1) * LANES)
        q = q_ref[:, sl]
        k = jnp.concatenate([k2_ref[:, sl], k1_ref[:, sl], k0_ref[:, sl]], axis=0)
        v = jnp.concatenate([v2_ref[:, sl], v1_ref[:, sl], v0_ref[:, sl]], axis=0)
        zero = jnp.zeros_like(q)
        q2 = jnp.concatenate([jnp.where(first, q, zero), jnp.where(first, zero, q)], axis=0)
        s = _dot_nt(q2, k) + bias_ref[2 * p:2 * p + 2].reshape(2 * tq, win)
        m = jnp.max(s, axis=1, keepdims=True)
        e = jnp.exp2(s - m)
        l = jnp.sum(e, axis=1, keepdims=True)
        o = _dot(e.astype(BF16), v) / l
        o_ref[:, sl] = jnp.where(first, o[:tq], o[tq:]).astype(BF16)


def _chunk_attention(qb, kb, vb, bias, layer):
    B, S, _ = qb.shape
    tq = TQ_CA
    n = S // tq
    cur = pl.BlockSpec((None, tq, CA_DIM), lambda b, i: (b, i, 0))
    prev1 = pl.BlockSpec((None, tq, CA_DIM), lambda b, i: (b, jnp.maximum(i - 1, 0), 0))
    prev2 = pl.BlockSpec((None, tq, CA_DIM), lambda b, i: (b, jnp.maximum(i - 2, 0), 0))
    return pl.pallas_call(
        _chunk_attn_kernel,
        grid=(B, n),
        in_specs=[cur, prev2, prev1, cur, prev2, prev1, cur,
                  pl.BlockSpec((None, None) + bias.shape[2:],
                               lambda b, i: (layer, jnp.minimum(i, N_CA_LEAD), 0, 0, 0))],
        out_specs=cur,
        out_shape=jax.ShapeDtypeStruct((B, S, CA_DIM), BF16),
        compiler_params=_params("parallel", "parallel"),
        name="chunk_attention",
    )(qb, kb, kb, kb, vb, vb, vb, bias)


def _post_attn_kernel(x_ref, a_ref, b_ref, woa_ref, wob_ref, fn_ref, *rest, route):
    if route:
        wr_ref, x2_out, h_out, meta_out, gate_out, cnt_out, run_scr = rest
    else:
        x2_out, h_out = rest
    x2 = x_ref[...] + _dot(a_ref[...], woa_ref[...]) + _dot(b_ref[...], wob_ref[...])
    x2_out[...] = x2
    h = _rms(x2, fn_ref[...])
    h_out[...] = h.astype(h_out.dtype)
    if not route:
        return

    tm = x2.shape[0]
    step = pl.program_id(0)

    @pl.when(step == 0)
    def _():
        run_scr[...] = jnp.zeros(run_scr.shape, F32)

    lane = lax.broadcasted_iota(jnp.int32, (tm, LANES), 1)
    h_hi = h.astype(BF16)
    h_lo = (h - h_hi.astype(F32)).astype(BF16)
    logits = _dot(h_hi, wr_ref[0]) + (_dot(h_lo, wr_ref[0]) + _dot(h_hi, wr_ref[1]))
    logits = jnp.where(lane < N_EXPERTS, logits, -jnp.inf)
    v1 = jnp.max(logits, axis=1, keepdims=True)
    e1 = jnp.min(jnp.where(logits == v1, lane, LANES), axis=1, keepdims=True)
    rest_logits = jnp.where(lane == e1, -jnp.inf, logits)
    v2 = jnp.max(rest_logits, axis=1, keepdims=True)
    e2 = jnp.min(jnp.where(rest_logits == v2, lane, LANES), axis=1, keepdims=True)
    w2 = jnp.exp(v2 - v1)
    g1 = 1.0 / (1.0 + w2)
    g2 = w2 / (1.0 + w2)

    sel1 = lane == e1
    sel2 = lane == e2
    cnt = jnp.where(sel1 | sel2, 1.0, 0.0)
    r_i = lax.broadcasted_iota(jnp.int32, (tm, tm), 0)
    c_i = lax.broadcasted_iota(jnp.int32, (tm, tm), 1)
    lower = jnp.where(c_i < r_i, 1.0, 0.0).astype(BF16)
    before = _dot(lower, cnt.astype(BF16)) + run_scr[...]
    rank1 = jnp.sum(jnp.where(sel1, before, 0.0), axis=1, keepdims=True)
    rank2 = jnp.sum(jnp.where(sel2, before, 0.0), axis=1, keepdims=True)
    run_scr[...] = run_scr[...] + jnp.sum(cnt, axis=0, keepdims=True)
    cnt_out[...] = run_scr[...]

    meta = jnp.where(lane == 0, e1, jnp.where(lane == 1, e2, jnp.where(
        lane == 2, rank1.astype(jnp.int32), jnp.where(lane == 3, rank2.astype(jnp.int32), 0))))
    meta_out[...] = meta
    gate_out[...] = jnp.where(lane == 0, g1, jnp.where(lane == 1, g2, 0.0))


def _post_attn(x, a, b, lw, route):
    T, D = x.shape
    tm = TM_PROJ
    half = a.shape[1]
    tok = lambda w: pl.BlockSpec((tm, w), lambda i: (i, 0))
    full = lambda arr: pl.BlockSpec(arr.shape, lambda i: (0,) * arr.ndim)
    ins = [x, a, b, lw['w_out_a'], lw['w_out_b'], lw['ffn_norm']]
    in_specs = [tok(D), tok(half), tok(half), full(ins[3]), full(ins[4]), full(ins[5])]
    out_specs = [tok(D), tok(D)]
    out_shape = [jax.ShapeDtypeStruct((T, D), F32),
                 jax.ShapeDtypeStruct((T, D), F32 if route else BF16)]
    scratch = []
    if route:
        ins.append(lw['w_router'])
        in_specs.append(full(lw['w_router']))
        out_specs += [tok(LANES), tok(LANES), pl.BlockSpec((1, LANES), lambda i: (0, 0))]
        out_shape += [jax.ShapeDtypeStruct((T, LANES), jnp.int32),
                      jax.ShapeDtypeStruct((T, LANES), F32),
                      jax.ShapeDtypeStruct((1, LANES), F32)]
        scratch = [pltpu.VMEM((1, LANES), F32)]
    return pl.pallas_call(
        functools.partial(_post_attn_kernel, route=route),
        grid=(T // tm,),
        in_specs=in_specs,
        out_specs=out_specs,
        out_shape=out_shape,
        scratch_shapes=scratch,
        compiler_params=_params("arbitrary"),
        name="post_attn_route" if route else "post_attn",
    )(*ins)


def _ffn_kernel(be_ref, nv_ref, h_ref, *rest, residual):
    if residual:
        res_ref, w1_ref, w3_ref, w2_ref, o_ref = rest
    else:
        w1_ref, w3_ref, w2_ref, o_ref = rest
    i = pl.program_id(0)
    f = pl.program_id(1)

    @pl.when(f == 0)
    def _():
        o_ref[...] = res_ref[...] if residual else jnp.zeros(o_ref.shape, F32)

    @pl.when(i < nv_ref[0])
    def _():
        h = h_ref[...].astype(BF16)
        a = _dot(h, w1_ref[...])
        g = _dot(h, w3_ref[...])
        act = (a * jax.nn.sigmoid(a) * g).astype(BF16)
        o_ref[...] += _dot(act, w2_ref[...])


def _ffn(h, res, w1, w3, w2, layer, block_expert, n_valid):
    R, D = h.shape
    tm, tf = TM_FFN, TF_FFN
    n_f = FFN_DIM // tf
    last = lambda i, nv: jnp.minimum(i, nv[0] - 1)
    row = pl.BlockSpec((tm, D), lambda i, f, be, nv: (last(i, nv), 0))
    out = pl.BlockSpec((tm, D), lambda i, f, be, nv: (i, 0))
    fcl = lambda i, f, nv: jnp.where(i < nv[0], f, n_f - 1)
    w13 = pl.BlockSpec((None, None, D, tf),
                       lambda i, f, be, nv: (layer, be[last(i, nv)], 0, fcl(i, f, nv)))
    w2s = pl.BlockSpec((None, None, tf, D),
                       lambda i, f, be, nv: (layer, be[last(i, nv)], fcl(i, f, nv), 0))
    residual = res is not None
    ins = [h] + ([res] if residual else []) + [w1, w3, w2]
    in_specs = [row] + ([out] if residual else []) + [w13, w13, w2s]
    return pl.pallas_call(
        functools.partial(_ffn_kernel, residual=residual),
        grid_spec=pltpu.PrefetchScalarGridSpec(
            num_scalar_prefetch=2, grid=(R // tm, n_f), in_specs=in_specs, out_specs=out),
        out_shape=jax.ShapeDtypeStruct((R, D), F32),
        compiler_params=_params("parallel", "arbitrary"),
        name="swiglu_res" if residual else "swiglu_expert",
    )(block_expert, n_valid, *ins)


def _start_rows(copy, n):
    def body(r, c):
        for k in range(TOP_K):
            copy(r, k).start(priority=k)
        return c
    lax.fori_loop(0, n, body, 0, unroll=MOVE_UNROLL)


def _wait_rows(copy, n):
    def body(r, c):
        for k in range(TOP_K):
            copy(r, k).wait()
        return c
    lax.fori_loop(0, n, body, 0, unroll=MOVE_UNROLL)


def _dispatch_kernel(dest_ref, h_ref, xs_in, xs_out, sem):
    del xs_in

    def copy(r, k):
        return pltpu.make_async_copy(h_ref.at[pl.ds(r, 1)],
                                     xs_out.at[pl.ds(dest_ref[TOP_K * r + k], 1)], sem)

    _start_rows(copy, TM_DISPATCH)
    _wait_rows(copy, TM_DISPATCH)


def _dispatch(h, dest_flat, n_rows):
    T, D = h.shape
    tm = TM_DISPATCH
    zeros = jnp.zeros((n_rows, D), h.dtype)
    return pl.pallas_call(
        _dispatch_kernel,
        grid=(T // tm,),
        in_specs=[pl.BlockSpec((TOP_K * tm,), lambda i: (i,), memory_space=pltpu.SMEM),
                  pl.BlockSpec((tm, D), lambda i: (i, 0)),
                  pl.BlockSpec(memory_space=pl.ANY)],
        out_specs=pl.BlockSpec(memory_space=pl.ANY),
        out_shape=jax.ShapeDtypeStruct((n_rows, D), h.dtype),
        scratch_shapes=[pltpu.SemaphoreType.DMA(())],
        input_output_aliases={2: 0},
        compiler_params=_params("arbitrary"),
        name="moe_dispatch",
    )(dest_flat, h, zeros)


def _combine_kernel(dest_ref, next_ref, x_ref, gate_ref, g_ref, ys_hbm, o_ref, buf, sems, *, final):
    tm = TM_MOVE
    i = pl.program_id(0)
    slot = i % 2

    def copies(idx_ref, s):
        def copy(r, k):
            return pltpu.make_async_copy(ys_hbm.at[pl.ds(idx_ref[TOP_K * r + k], 1)],
                                         buf.at[s, k, pl.ds(r, 1)], sems.at[s])
        return copy

    @pl.when(i == 0)
    def _():
        _start_rows(copies(dest_ref, slot), tm)

    @pl.when(i + 1 < pl.num_programs(0))
    def _():
        _start_rows(copies(next_ref, 1 - slot), tm)

    _wait_rows(copies(dest_ref, slot), tm)
    gates = gate_ref[...]
    y = x_ref[...] + (buf[slot, 0] * gates[:, 0:1] + buf[slot, 1] * gates[:, 1:2])
    o_ref[...] = _rms(y, g_ref[...]) if final else y


def _combine(x, ys, gates, dest_flat, final_g):
    T, D = x.shape
    tm = TM_MOVE
    n = T // tm
    final = final_g is not None
    g = final_g if final else jnp.ones((1, D), F32)
    return pl.pallas_call(
        functools.partial(_combine_kernel, final=final),
        grid=(n,),
        in_specs=[pl.BlockSpec((TOP_K * tm,), lambda i: (i,), memory_space=pltpu.SMEM),
                  pl.BlockSpec((TOP_K * tm,), lambda i: (jnp.minimum(i + 1, n - 1),),
                               memory_space=pltpu.SMEM),
                  pl.BlockSpec((tm, D), lambda i: (i, 0)),
                  pl.BlockSpec((tm, LANES), lambda i: (i, 0)),
                  pl.BlockSpec((1, D), lambda i: (0, 0)),
                  pl.BlockSpec(memory_space=pl.ANY)],
        out_specs=pl.BlockSpec((tm, D), lambda i: (i, 0)),
        out_shape=jax.ShapeDtypeStruct((T, D), F32),
        scratch_shapes=[pltpu.VMEM((2, TOP_K, tm, D), F32), pltpu.SemaphoreType.DMA((2,))],
        compiler_params=_params("arbitrary"),
        name="moe_combine_final" if final else "moe_combine",
    )(dest_flat, dest_flat, x, gates, g, ys)


def _final_norm_kernel(x_ref, g_ref, o_ref):
    o_ref[...] = _rms(x_ref[...], g_ref[...])


def _final_norm(x, g):
    T, D = x.shape
    tm = TM_PROJ
    return pl.pallas_call(
        _final_norm_kernel,
        grid=(T // tm,),
        in_specs=[pl.BlockSpec((tm, D), lambda i: (i, 0)), pl.BlockSpec((1, D), lambda i: (0, 0))],
        out_specs=pl.BlockSpec((tm, D), lambda i: (i, 0)),
        out_shape=jax.ShapeDtypeStruct((T, D), F32),
        compiler_params=_params("parallel"),
        name="final_norm",
    )(x, g)


def _rope_tables(seq):
    inv = 1.0 / (ROPE_THETA ** (jnp.arange(0, MLA_ROPE, 2, dtype=F32) / MLA_ROPE))
    ang = jnp.arange(seq, dtype=F32)[:, None] * inv[None, :]
    cos, sin = jnp.cos(ang), jnp.sin(ang)
    ones = jnp.ones((seq, MLA_NOPE), F32)
    z_nope = jnp.zeros((seq, MLA_NOPE), F32)
    z_pad = jnp.zeros((seq, LANES - MLA_NOPE - MLA_ROPE), F32)
    scale = (MLA_NOPE + MLA_ROPE) ** -0.5 * np.log2(np.e)
    cq =jnp.concatenate([ones, cos, cos, z_pad], axis=1) * scale
    sq = jnp.concatenate([z_nope, sin, sin, z_pad], axis=1) * scale
    ck = jnp.concatenate([z_nope, cos, cos, z_pad], axis=1)
    sk = jnp.concatenate([z_nope, sin, sin, z_pad], axis=1)
    return cq, sq, ck, sk


def _layer_weights(i, attn_norm, w_in, q_norm, w_uq, kv_norm, w_ukv, w_out, ffn_norm):
    w = w_in[i]
    o = np.cumsum([0, Q_LORA, KV_LORA, MLA_ROPE])
    w_cq, w_ckv, w_kr = (w[:, o[j]:o[j + 1]] for j in range(3))
    w_qkvb = w[:, o[3]:]
    d = w.shape[0]
    x1, x2 = w_kr[:, :HALF_ROPE], w_kr[:, HALF_ROPE:]
    z_nope = jnp.zeros((d, MLA_NOPE), F32)
    z_pad = jnp.zeros((d, LANES - MLA_NOPE - MLA_ROPE), F32)
    w_kr2 = jnp.concatenate([z_nope, x1, x2, z_pad, z_nope, -x2, x1, z_pad], axis=1)

    wq = w_uq[i].reshape(Q_LORA, MLA_HEADS, MLA_NOPE + MLA_ROPE)
    nope, q1, q2 = wq[..., :MLA_NOPE], wq[..., MLA_NOPE:MLA_NOPE + HALF_ROPE], wq[..., MLA_NOPE + HALF_ROPE:]
    zq_nope = jnp.zeros((Q_LORA, MLA_HEADS, MLA_NOPE), F32)
    zq_pad = jnp.zeros((Q_LORA, MLA_HEADS, LANES - MLA_NOPE - MLA_ROPE), F32)
    pre = jnp.concatenate([nope, q1, q2, zq_pad], axis=-1).reshape(Q_LORA, MLA_HEADS * LANES)
    swap = jnp.concatenate([zq_nope, -q2, q1, zq_pad], axis=-1).reshape(Q_LORA, MLA_HEADS * LANES)

    wkv = w_ukv[i].reshape(KV_LORA, MLA_HEADS, MLA_NOPE + MLA_V)
    zk = jnp.zeros((KV_LORA, MLA_HEADS, LANES - MLA_NOPE), F32)
    w_uk = jnp.concatenate([wkv[..., :MLA_NOPE], zk], axis=-1).reshape(KV_LORA, MLA_HEADS * LANES)
    zv = jnp.zeros((KV_LORA, MLA_HEADS, LANES - MLA_V), F32)
    w_uv = jnp.concatenate([wkv[..., MLA_NOPE:], zv], axis=-1).reshape(KV_LORA, MLA_HEADS * LANES)
    half = MLA_HEADS * MLA_V
    return {
        'attn_norm': attn_norm[i][None], 'w_cq': w_cq.astype(BF16), 'w_ckv': w_ckv.astype(BF16),
        'w_kr': w_kr2.astype(BF16), 'w_qkvb': w_qkvb.astype(BF16), 'q_norm': q_norm[i][None],
        'w_uq': jnp.concatenate([pre, swap], axis=1).astype(BF16), 'kv_norm': kv_norm[i][None],
        'w_uk': w_uk.astype(BF16), 'w_uv': w_uv.astype(BF16),
        'w_out_a': w_out[i][:half].astype(BF16), 'w_out_b': w_out[i][half:].astype(BF16),
        'ffn_norm': ffn_norm[i][None],
    }


def _moe_layout(meta, counts):
    T = meta.shape[0]
    blk = TM_FFN
    cnt = counts[0, :N_EXPERTS].astype(jnp.int32)
    padded = (cnt + blk - 1) // blk * blk
    pends = jnp.cumsum(padded)
    pstarts = pends - padded
    dest = pstarts[meta[:, :TOP_K]] + meta[:, TOP_K:2 * TOP_K]
    n_rows = T * TOP_K + N_EXPERTS * blk
    n_blocks = n_rows // blk
    block_expert = jnp.minimum(
        jnp.searchsorted(pends, jnp.arange(n_blocks, dtype=jnp.int32) * blk, side='right'),
        N_EXPERTS - 1).astype(jnp.int32)
    n_valid = (pends[-1:] // blk).astype(jnp.int32)
    return dest.reshape(-1).astype(jnp.int32), block_expert, n_valid, n_rows


def kernel(x, attn_norm, w_in, q_norm, w_uq, kv_norm, w_ukv, rel_bias, w_out, ffn_norm, dense_w1,
           dense_w3, dense_w2, w_router, moe_w1, moe_w3, moe_w2, final_norm):
    B, S, D = x.shape
    T = B * S
    depth = w_in.shape[0]
    tabs = _rope_tables(S)
    bias = _bias_tiles(_rel_bias_band(rel_bias))
    dense_be = jnp.zeros((T // TM_FFN,), jnp.int32)
    dense_nv = jnp.full((1,), T // TM_FFN, jnp.int32)
    dense_w = [w.astype(BF16)[:, None] for w in (dense_w1, dense_w3, dense_w2)]
    moe_w = [w.astype(BF16) for w in (moe_w1, moe_w3, moe_w2)]
    out = None
    for i in range(depth):
        lw = _layer_weights(i, attn_norm, w_in, q_norm, w_uq, kv_norm, w_ukv, w_out, ffn_norm)
        q, k, v, qb, kb, vb = _pre_attn(x, lw, tabs)
        a = _mla_attention(q, k, v)
        b = _chunk_attention(qb, kb, vb, bias, i)
        xt = x.reshape(T, D)
        a2, b2 = a.reshape(T, -1), b.reshape(T, -1)
        j = i // 2
        last = i == depth - 1
        if i % 2 == 0:
            x2, h = _post_attn(xt, a2, b2, lw, route=False)
            y = _ffn(h, x2, *dense_w, j, dense_be, dense_nv)
            if last:
                out = _final_norm(y, final_norm[None])
        else:
            router = jnp.zeros((D, LANES), F32).at[:, :N_EXPERTS].set(w_router[j])
            router_hi = router.astype(BF16)
            router_lo = (router - router_hi.astype(F32)).astype(BF16)
            lw['w_router'] = jnp.stack([router_hi, router_lo])
            x2, h, meta, gates, counts = _post_attn(xt, a2, b2, lw, route=True)
            dest, block_expert, n_valid, n_rows = _moe_layout(meta, counts)
            xs = _dispatch(h, dest, n_rows)
            ys = _ffn(xs, None, *moe_w, j, block_expert, n_valid)
            y = _combine(x2, ys, gates, dest, final_norm[None] if last else None)
            if last:
                out = y
        x = y.reshape(B, S, D)
    return out.reshape(B, S, D)
```

```python
import functools

import jax
import jax.numpy as jnp
import numpy as np
from jax import lax
from jax.experimental import pallas as pl
from jax.experimental.pallas import tpu as pltpu

D_MODEL = 1024
CHUNK = 64
MLA_HEADS = 8
MLA_NOPE = 64
MLA_ROPE = 32
MLA_V = 64
Q_LORA = 384
KV_LORA = 256
ROPE_THETA = 10000.0
CA_HEADS = 8
CA_HEAD_DIM = 64
LEFT_CHUNKS = 8
REL_CLIP = 128
FFN_DIM = 3584
N_EXPERTS = 8
TOP_K = 2
EPS = 1e-6
NEG_INF = -1e30

CA_DIM = CA_HEADS * CA_HEAD_DIM
BAND = LEFT_CHUNKS + 1
HALF_ROPE = MLA_ROPE // 2
LANES = 128
SUBLANES = 8
assert D_MODEL == SUBLANES * LANES
VMEM_LIMIT = 56 * 1024 * 1024

TM_PROJ = 512
ROUTE_CHUNK = 128
TQ_MLA = 1024
TK_MLA = 512
TQ_CA = 256
CA_WIN = LEFT_CHUNKS * CHUNK + TQ_CA
N_CA_LEAD = LEFT_CHUNKS * CHUNK // TQ_CA
TM_FFN = 512
TF_FFN = 1792
TM_MOVE = 256
MOVE_UNROLL = 8

BF16 = jnp.bfloat16
F32 = jnp.float32


def _params(*sem):
    return pltpu.CompilerParams(dimension_semantics=sem, vmem_limit_bytes=VMEM_LIMIT)


def _rms(x, g):
    ms = jnp.mean(x * x, axis=-1, keepdims=True)
    return x * lax.rsqrt(ms + EPS) * g


def _dot(a, b):
    return jnp.dot(a, b, preferred_element_type=F32)


def _store_row_tiles(ref, value):
    n = value.shape[0]
    for s in range(SUBLANES):
        ref[pl.ds(s, n, stride=SUBLANES), :] = value[:, s * LANES:(s + 1) * LANES]


def _load_row_tiles(ref, n):
    return jnp.concatenate([ref[pl.ds(s, n, stride=SUBLANES), :] for s in range(SUBLANES)], axis=1)


def _dot_nt(a, b):
    return lax.dot_general(a, b, (((1,), (1,)), ((), ())), preferred_element_type=F32)


def _pre_attn_kernel(x_ref, an_ref, wcq_ref, wckv_ref, wkr_ref, wqkvb_ref, qn_ref, wuq_ref,
                     kvn_ref, wuk_ref, wuv_ref, cq_ref, sq_ref, ck_ref, sk_ref,
                     q_out, k_out, v_out, qb_out, kb_out, vb_out):
    h = _rms(x_ref[...], an_ref[...]).astype(BF16)
    c_q = _dot(h, wcq_ref[...])
    c_kv = _dot(h, wckv_ref[...])
    kr2 = _dot(h, wkr_ref[...])
    qkvb = _dot(h, wqkvb_ref[...])
    qb_out[...] = (qkvb[:, :CA_DIM] * (CA_HEAD_DIM ** -0.5 * np.log2(np.e))).astype(BF16)
    kb_out[...] = qkvb[:, CA_DIM:2 * CA_DIM].astype(BF16)
    vb_out[...] = qkvb[:, 2 * CA_DIM:].astype(BF16)

    q2 = _dot(_rms(c_q, qn_ref[...]).astype(BF16), wuq_ref[...])
    cq, sq = cq_ref[...], sq_ref[...]
    width = MLA_HEADS * LANES
    for hd in range(MLA_HEADS):
        lo = hd * LANES
        q_out[hd] = (q2[:, lo:lo + LANES] * cq + q2[:, width + lo:width + lo + LANES] * sq).astype(BF16)

    ckvn = _rms(c_kv, kvn_ref[...]).astype(BF16)
    kn = _dot(ckvn, wuk_ref[...])
    vv = _dot(ckvn, wuv_ref[...])
    kr = kr2[:, :LANES] * ck_ref[...] + kr2[:, LANES:] * sk_ref[...]
    lane = lax.broadcasted_iota(jnp.int32, kr.shape, 1)
    one = jnp.where(lane == MLA_V, 1.0, 0.0)
    for hd in range(MLA_HEADS):
        k_out[hd] = (kn[:, hd * LANES:(hd + 1) * LANES] + kr).astype(BF16)
        v_out[hd] = (vv[:, hd * LANES:(hd + 1) * LANES] + one).astype(BF16)


def _pre_attn(x, lw, tabs, layer):
    B, S, D = x.shape
    tm = TM_PROJ
    n_s = S // tm
    full = lambda a: pl.BlockSpec((None,) + a.shape[1:], lambda b, i: (layer,) + (0,) * (a.ndim - 1))
    tab = pl.BlockSpec((tm, LANES), lambda b, i: (i, 0))
    tok = lambda w: pl.BlockSpec((None, tm, w), lambda b, i: (b, i, 0))
    heads = lambda n: pl.BlockSpec((None, n, tm, LANES), lambda b, i: (b, 0, i, 0))
    weights = [lw['attn_norm'], lw['w_cq'], lw['w_ckv'], lw['w_kr'], lw['w_qkvb'], lw['q_norm'],
               lw['w_uq'], lw['kv_norm'], lw['w_uk'], lw['w_uv']]
    return pl.pallas_call(
        _pre_attn_kernel,
        grid=(B, n_s),
        in_specs=[tok(D)] + [full(w) for w in weights] + [tab] * 4,
        out_specs=[heads(MLA_HEADS), heads(MLA_HEADS), heads(MLA_HEADS),
                   tok(CA_DIM), tok(CA_DIM), tok(CA_DIM)],
        out_shape=[jax.ShapeDtypeStruct((B, MLA_HEADS, S, LANES), BF16),
                   jax.ShapeDtypeStruct((B, MLA_HEADS, S, LANES), BF16),
                   jax.ShapeDtypeStruct((B, MLA_HEADS, S, LANES), BF16),
                   jax.ShapeDtypeStruct((B, S, CA_DIM), BF16),
                   jax.ShapeDtypeStruct((B, S, CA_DIM), BF16),
                   jax.ShapeDtypeStruct((B, S, CA_DIM), BF16)],
        compiler_params=_params("parallel", "parallel"),
        name="pre_attn",
    )(x, *weights, *tabs)


def _mla_kernel(qi_ref, kj_ref, q_ref, k_ref, v_ref, o_ref, m_scr, acc_scr, *, tq, tk):
    t = pl.program_id(1)
    kj = kj_ref[t]
    ratio = tq // tk
    diag = kj - qi_ref[t] * ratio

    @pl.when(kj == 0)
    def _():
        m_scr[...] = jnp.full(m_scr.shape, NEG_INF, F32)
        acc_scr[...] = jnp.zeros(acc_scr.shape, F32)

    def update(hd, lo, n, visible):
        s = _dot_nt(q_ref[hd, lo:lo + n], k_ref[hd])
        if visible is not None:
            s = jnp.where(visible, s, NEG_INF)
        m_prev = m_scr[hd, lo:lo + n]
        m_new = jnp.maximum(m_prev, jnp.max(s, axis=1, keepdims=True))
        p = jnp.exp2((s - jnp.tile(m_new, (1, tk // LANES))).astype(BF16))
        m_scr[hd, lo:lo + n] = m_new
        acc_scr[hd, lo:lo + n] = (acc_scr[hd, lo:lo + n] * jnp.exp2(m_prev - m_new)
                                  + _dot(p, v_ref[hd]))

    @pl.when(diag < 0)
    def _():
        for hd in range(MLA_HEADS):
            update(hd, 0, tq, None)

    for d in range(ratio):
        @pl.when(diag == d)
        def _():
            row = lax.broadcasted_iota(jnp.int32, (tk, tk), 0) // CHUNK
            col = lax.broadcasted_iota(jnp.int32, (tk, tk), 1) // CHUNK
            visible = col <= row
            below = (d + 1) * tk
            for hd in range(MLA_HEADS):
                update(hd, d * tk, tk, visible)
                if below < tq:
                    update(hd, below, tq - below, None)

    @pl.when(diag == ratio - 1)
    def _():
        lane = lax.broadcasted_iota(jnp.int32, (tq, LANES), 1)
        for p in range(MLA_HEADS // 2):
            even, odd = acc_scr[2 * p], acc_scr[2 * p + 1]
            even = even / even[:, MLA_V:MLA_V + 1]
            odd = odd / odd[:, MLA_V:MLA_V + 1]
            pair = jnp.where(lane < MLA_V, even, pltpu.roll(odd, MLA_V, axis=1))
            o_ref[:, p * LANES:(p + 1) * LANES] = pair.astype(BF16)


def _mla_attention(q, k, v):
    B, H, S, _ = q.shape
    tq, tk = TQ_MLA, TK_MLA
    ratio = tq // tk
    pairs = [(i, j) for i in range(S // tq) for j in range(ratio * (i + 1))]
    qi = jnp.asarray(np.array([p[0] for p in pairs], np.int32))
    kj = jnp.asarray(np.array([p[1] for p in pairs], np.int32))
    kv_spec = pl.BlockSpec((None, H, tk, LANES), lambda b, t, qi, kj: (b, 0, kj[t], 0))
    grid_spec = pltpu.PrefetchScalarGridSpec(
        num_scalar_prefetch=2,
        grid=(B, len(pairs)),
        in_specs=[pl.BlockSpec((None, H, tq, LANES), lambda b, t, qi, kj: (b, 0, qi[t], 0)),
                  kv_spec, kv_spec],
        out_specs=pl.BlockSpec((None, tq, H * MLA_V), lambda b, t, qi, kj: (b, qi[t], 0)),
        scratch_shapes=[pltpu.VMEM((H, tq, LANES), F32)] * 2,
    )
    return pl.pallas_call(
        functools.partial(_mla_kernel, tq=tq, tk=tk),
        grid_spec=grid_spec,
        out_shape=jax.ShapeDtypeStruct((B, S, H * MLA_V), BF16),
        compiler_params=_params("parallel", "arbitrary"),
        name="mla_attention",
    )(qi, kj, q, k, v)


def _rel_bias_kernel(tab_ref, o_ref):
    r = lax.broadcasted_iota(jnp.int32, (CHUNK, BAND * CHUNK), 0)
    c = lax.broadcasted_iota(jnp.int32, (CHUNK, BAND * CHUNK), 1)
    rel = jnp.clip(LEFT_CHUNKS * CHUNK + r - c, -REL_CLIP, REL_CLIP) + REL_CLIP
    t_lo = REL_CLIP + max(-REL_CLIP, LEFT_CHUNKS * CHUNK - (BAND * CHUNK - 1))
    t_hi = REL_CLIP + min(REL_CLIP, LEFT_CHUNKS * CHUNK + CHUNK - 1)
    for hd in range(CA_HEADS):
        def body(t, acc):
            return jnp.where(rel == t, tab_ref[hd, t], acc)
        o_ref[hd] = lax.fori_loop(t_lo, t_hi + 1, body, jnp.zeros(rel.shape, F32))


def _rel_bias_band(rel_bias):
    depth = rel_bias.shape[0]
    return pl.pallas_call(
        _rel_bias_kernel,
        grid=(depth,),
        in_specs=[pl.BlockSpec((None, CA_HEADS, 2 * REL_CLIP + 1), lambda d: (d, 0, 0),
                               memory_space=pltpu.SMEM)],
        out_specs=pl.BlockSpec((None, CA_HEADS, CHUNK, BAND * CHUNK), lambda d: (d, 0, 0, 0)),
        out_shape=jax.ShapeDtypeStruct((depth, CA_HEADS, CHUNK, BAND * CHUNK), F32),
        compiler_params=_params("parallel"),
        name="rel_bias_band",
    )(rel_bias)


def _bias_tiles(band):
    depth = band.shape[0]
    n_qc = TQ_CA // CHUNK
    n_kc = CA_WIN // CHUNK
    neg = jnp.full((depth, CA_HEADS, CHUNK, CHUNK), NEG_INF, F32)
    rows = []
    for rc in range(n_qc):
        blocks = []
        for cc in range(n_kc):
            j = cc - rc
            blocks.append(band[..., j * CHUNK:(j + 1) * CHUNK] if 0 <= j < BAND else neg)
        rows.append(jnp.concatenate(blocks, axis=-1))
    tile = jnp.concatenate(rows, axis=-2) * np.log2(np.e)
    col = jnp.arange(CA_WIN)
    variants = [jnp.where(col >= LEFT_CHUNKS * CHUNK - i * TQ_CA, tile, NEG_INF)
                for i in range(N_CA_LEAD)] + [tile]
    return jnp.stack(variants, axis=1)


def _chunk_attn_kernel(q_ref, k2_ref, k1_ref, k0_ref, v2_ref, v1_ref, v0_ref, bias_ref, o_ref):
    tq, win = TQ_CA, CA_WIN
    lane_q = lax.broadcasted_iota(jnp.int32, (tq, LANES), 1)
    first = lane_q < CA_HEAD_DIM
    for p in range(CA_HEADS // 2):
        sl = slice(p * LANES, (p + 1) * LANES)
        q = q_ref[:, sl]
        k = jnp.concatenate([k2_ref[:, sl], k1_ref[:, sl], k0_ref[:, sl]], axis=0)
        v = jnp.concatenate([v2_ref[:, sl], v1_ref[:, sl], v0_ref[:, sl]], axis=0)
        zero = jnp.zeros_like(q)
        q2 = jnp.concatenate([jnp.where(first, q, zero), jnp.where(first, zero, q)], axis=0)
        s = _dot_nt(q2, k) + bias_ref[2 * p:2 * p + 2].reshape(2 * tq, win)
        m = jnp.max(s, axis=1, keepdims=True)
        e = jnp.exp2(s - m)
        l = jnp.sum(e, axis=1, keepdims=True)
        o = _dot(e.astype(BF16), v) / l
        o_ref[:, sl] = jnp.where(first, o[:tq], o[tq:]).astype(BF16)


def _chunk_attention(qb, kb, vb, bias, layer):
    B, S, _ = qb.shape
    tq = TQ_CA
    n = S // tq
    cur = pl.BlockSpec((None, tq, CA_DIM), lambda b, i: (b, i, 0))
    prev1 = pl.BlockSpec((None, tq, CA_DIM), lambda b, i: (b, jnp.maximum(i - 1, 0), 0))
    prev2 = pl.BlockSpec((None, tq, CA_DIM), lambda b, i: (b, jnp.maximum(i - 2, 0), 0))
    return pl.pallas_call(
        _chunk_attn_kernel,
        grid=(B, n),
        in_specs=[cur, prev2, prev1, cur, prev2, prev1, cur,
                  pl.BlockSpec((None, None) + bias.shape[2:],
                               lambda b, i: (layer, jnp.minimum(i, N_CA_LEAD), 0, 0, 0))],
        out_specs=cur,
        out_shape=jax.ShapeDtypeStruct((B, S, CA_DIM), BF16),
        compiler_params=_params("parallel", "parallel"),
        name="chunk_attention",
    )(qb, kb, kb, kb, vb, vb, vb, bias)


def _post_attn_kernel(x_ref, a_ref, b_ref, woa_ref, wob_ref, fn_ref, *rest, route):
    if route:
        wr_ref, x2_out, h_out, meta_out, gate_out, cnt_out, run_scr = rest
    else:
        x2_out, h_out = rest
    x2 = x_ref[...] + _dot(a_ref[...], woa_ref[...]) + _dot(b_ref[...], wob_ref[...])
    x2_out[...] = x2
    h = _rms(x2, fn_ref[...])
    if not route:
        h_out[...] = h.astype(h_out.dtype)
        return
    _store_row_tiles(h_out, h)

    tm = x2.shape[0]
    step = pl.program_id(0)

    @pl.when(step == 0)
    def _():
        run_scr[...] = jnp.zeros(run_scr.shape, F32)

    h_hi = h.astype(BF16)
    h_lo = (h - h_hi.astype(F32)).astype(BF16)
    all_logits = _dot(h_hi, wr_ref[0]) + (_dot(h_lo, wr_ref[0]) + _dot(h_hi, wr_ref[1]))

    rc = ROUTE_CHUNK
    lane = lax.broadcasted_iota(jnp.int32, (rc, LANES), 1)
    r_i = lax.broadcasted_iota(jnp.int32, (rc, rc), 0)
    c_i = lax.broadcasted_iota(jnp.int32, (rc, rc), 1)
    lower = jnp.where(c_i < r_i, 1.0, 0.0).astype(BF16)
    run = run_scr[...]
    for c in range(tm // rc):
        rows = slice(c * rc, (c + 1) * rc)
        logits = jnp.where(lane < N_EXPERTS, all_logits[rows], -jnp.inf)
        v1 = jnp.max(logits, axis=1, keepdims=True)
        e1 = jnp.min(jnp.where(logits == v1, lane, LANES), axis=1, keepdims=True)
        rest_logits = jnp.where(lane == e1, -jnp.inf, logits)
        v2 = jnp.max(rest_logits, axis=1, keepdims=True)
        e2 = jnp.min(jnp.where(rest_logits == v2, lane, LANES), axis=1, keepdims=True)
        w2 = jnp.exp(v2 - v1)
        g1 = 1.0 / (1.0 + w2)
        g2 = w2 / (1.0 + w2)

        sel1 = lane == e1
        sel2 = lane == e2
        cnt = jnp.where(sel1 | sel2, 1.0, 0.0)
        before = _dot(lower, cnt.astype(BF16)) + run
        rank1 = jnp.sum(jnp.where(sel1, before, 0.0), axis=1, keepdims=True)
        rank2 = jnp.sum(jnp.where(sel2, before, 0.0), axis=1, keepdims=True)
        run = run + jnp.sum(cnt, axis=0, keepdims=True)

        meta_out[rows] = jnp.where(lane == 0, e1, jnp.where(lane == 1, e2, jnp.where(
            lane == 2, rank1.astype(jnp.int32), jnp.where(lane == 3, rank2.astype(jnp.int32), 0))))
        gate_out[rows] = jnp.where(lane == 0, g1, jnp.where(lane == 1, g2, 0.0))
    run_scr[...] = run
    cnt_out[...] = run


def _post_attn(x, a, b, lw, layer, route):
    T, D = x.shape
    tm = TM_PROJ
    half = a.shape[1]
    tok = lambda w: pl.BlockSpec((tm, w), lambda i: (i, 0))
    stacked = lambda arr, l: pl.BlockSpec((None,) + arr.shape[1:], lambda i: (l,) + (0,) * (arr.ndim - 1))
    full = lambda arr: stacked(arr, layer)
    ins = [x, a, b, lw['w_out_a'], lw['w_out_b'], lw['ffn_norm']]
    in_specs = [tok(D), tok(half), tok(half), full(ins[3]), full(ins[4]), full(ins[5])]
    if route:
        h_spec = pl.BlockSpec((tm * SUBLANES, LANES), lambda i: (i, 0))
        h_shape = jax.ShapeDtypeStruct((T * SUBLANES, LANES), F32)
    else:
        h_spec, h_shape = tok(D), jax.ShapeDtypeStruct((T, D), BF16)
    out_specs = [tok(D), h_spec]
    out_shape = [jax.ShapeDtypeStruct((T, D), F32), h_shape]
    scratch = []
    if route:
        ins.append(lw['w_router'])
        in_specs.append(stacked(lw['w_router'], layer // 2))
        out_specs += [tok(LANES), tok(LANES), pl.BlockSpec((1, LANES), lambda i: (0, 0))]
        out_shape += [jax.ShapeDtypeStruct((T, LANES), jnp.int32),
                      jax.ShapeDtypeStruct((T, LANES), F32),
                      jax.ShapeDtypeStruct((1, LANES), F32)]
        scratch = [pltpu.VMEM((1, LANES), F32)]
    return pl.pallas_call(
        functools.partial(_post_attn_kernel, route=route),
        grid=(T // tm,),
        in_specs=in_specs,
        out_specs=out_specs,
        out_shape=out_shape,
        scratch_shapes=scratch,
        compiler_params=_params("arbitrary"),
        name="post_attn_route" if route else "post_attn",
    )(*ins)


def _swiglu_accumulate(h, w1_ref, w3_ref, w2_ref, o_ref):
    a = _dot(h, w1_ref[...])
    g = _dot(h, w3_ref[...])
    act = (a * jax.nn.sigmoid(a) * g).astype(BF16)
    o_ref[...] += _dot(act, w2_ref[...])


def _ffn_kernel(h_ref, res_ref, w1_ref, w3_ref, w2_ref, o_ref):
    @pl.when(pl.program_id(1) == 0)
    def _():
        o_ref[...] = res_ref[...]

    _swiglu_accumulate(h_ref[...], w1_ref, w3_ref, w2_ref, o_ref)


def _ffn(h, res, w1, w3, w2, layer):
    R, D = h.shape
    tm, tf = TM_FFN, TF_FFN
    row = pl.BlockSpec((tm, D), lambda i, f: (i, 0))
    w13 = pl.BlockSpec((None, None, D, tf), lambda i, f: (layer, 0, 0, f))
    w2s = pl.BlockSpec((None, None, tf, D), lambda i, f: (layer, 0, f, 0))
    return pl.pallas_call(
        _ffn_kernel,
        grid=(R // tm, FFN_DIM // tf),
        in_specs=[row, row, w13, w13, w2s],
        out_specs=row,
        out_shape=jax.ShapeDtypeStruct((R, D), F32),
        compiler_params=_params("parallel", "arbitrary"),
        name="swiglu_res",
    )(h, res, w1, w3, w2)


def _moe_ffn_kernel(be_ref, nv_ref, tok_ref, nxt_ref, h_hbm, w1_ref, w3_ref, w2_ref, o_ref,
                    xbuf, acc, sems):
    del be_ref
    tm = TM_FFN
    i = pl.program_id(0)
    f = pl.program_id(1)
    n_f = pl.num_programs(1)
    n_valid = nv_ref[0]
    slot = i % 2
    share = tm // (FFN_DIM // TF_FFN)

    def row_copy(idx_ref, s, r):
        src = h_hbm.at[pl.ds(pl.multiple_of(idx_ref[r], SUBLANES), SUBLANES)]
        dst = xbuf.at[s, pl.ds(pl.multiple_of(r * SUBLANES, SUBLANES), SUBLANES)]
        return pltpu.make_async_copy(src, dst, sems.at[s])

    def wait_block(idx_ref, s):
        def body(r, c):
            row_copy(idx_ref, s, r).wait()
            return c
        lax.fori_loop(0, tm, body, 0, unroll=MOVE_UNROLL)

    @pl.when(f == 0)
    def _():
        acc[...] = jnp.zeros(acc.shape, F32)

    @pl.when((i == 0) & (f == 0))
    def _():
        def body(r, c):
            row_copy(tok_ref, 0, r).start()
            return c
        lax.fori_loop(0, tm, body, 0, unroll=MOVE_UNROLL)

    @pl.when((f == 0) & (i < n_valid))
    def _():
        wait_block(tok_ref, slot)

    @pl.when(i < n_valid)
    def _():
        for r in range(share):
            row_copy(nxt_ref, 1 - slot, f * share + r).start()
        h = _load_row_tiles(xbuf.at[slot], tm).astype(BF16)
        _swiglu_accumulate(h, w1_ref, w3_ref, w2_ref, acc)

    @pl.when(f == n_f - 1)
    def _():
        _store_row_tiles(o_ref, acc[...])

    @pl.when((f == n_f - 1) & (i == n_valid - 1))
    def _():
        wait_block(nxt_ref, 1 - slot)


def _moe_ffn(h, row_token, w1, w3, w2, layer, block_expert, n_valid):
    R = row_token.shape[0]
    D = D_MODEL
    tile_row = row_token * SUBLANES
    tm, tf = TM_FFN, TF_FFN
    n_blocks = R // tm
    n_f = FFN_DIM // tf
    last = lambda i, nv: jnp.minimum(i, nv[0] - 1)
    fcl = lambda i, f, nv: jnp.where(i < nv[0], f, n_f - 1)
    w13 = pl.BlockSpec((None, None, D, tf),
                       lambda i, f, be, nv: (layer, be[last(i, nv)], 0, fcl(i, f, nv)))
    w2s = pl.BlockSpec((None, None, tf, D),
                       lambda i, f, be, nv: (layer, be[last(i, nv)], fcl(i, f, nv), 0))
    tok = pl.BlockSpec((tm,), lambda i, f, be, nv: (i,), memory_space=pltpu.SMEM)
    nxt = pl.BlockSpec((tm,), lambda i, f, be, nv: (jnp.minimum(i + 1, n_blocks - 1),),
                       memory_space=pltpu.SMEM)
    return pl.pallas_call(
        _moe_ffn_kernel,
        grid_spec=pltpu.PrefetchScalarGridSpec(
            num_scalar_prefetch=2, grid=(n_blocks, n_f),
            in_specs=[tok, nxt, pl.BlockSpec(memory_space=pl.ANY), w13, w13, w2s],
            out_specs=pl.BlockSpec((tm * SUBLANES, LANES), lambda i, f, be, nv: (i, 0)),
            scratch_shapes=[pltpu.VMEM((2, tm * SUBLANES, LANES), F32), pltpu.VMEM((tm, D), F32),
                            pltpu.SemaphoreType.DMA((2,))]),
        out_shape=jax.ShapeDtypeStruct((R * SUBLANES, LANES), F32),
        compiler_params=_params("arbitrary", "arbitrary"),
        name="swiglu_expert",
    )(block_expert, n_valid, tile_row, tile_row, h, w1, w3, w2)


def _start_rows(copy, n):
    def body(r, c):
        for k in range(TOP_K):
            copy(r, k).start(priority=k)
        return c
    lax.fori_loop(0, n, body, 0, unroll=MOVE_UNROLL)


def _wait_rows(copy, n):
    def body(r, c):
        for k in range(TOP_K):
            copy(r, k).wait()
        return c
    lax.fori_loop(0, n, body, 0, unroll=MOVE_UNROLL)


def _combine_kernel(dest_ref, next_ref, x_ref, gate_ref, g_ref, ys_hbm, o_ref, buf, sems, *, final):
    tm = TM_MOVE
    i = pl.program_id(0)
    slot = i % 2

    def copies(idx_ref, s):
        def copy(r, k):
            src = ys_hbm.at[pl.ds(pl.multiple_of(idx_ref[TOP_K * r + k], SUBLANES), SUBLANES)]
            dst = buf.at[s, k, pl.ds(pl.multiple_of(r * SUBLANES, SUBLANES), SUBLANES)]
            return pltpu.make_async_copy(src, dst, sems.at[s])
        return copy

    @pl.when(i == 0)
    def _():
        _start_rows(copies(dest_ref, slot), tm)

    @pl.when(i + 1 < pl.num_programs(0))
    def _():
        _start_rows(copies(next_ref, 1 - slot), tm)

    _wait_rows(copies(dest_ref, slot), tm)
    gates = gate_ref[...]
    y = x_ref[...] + (_load_row_tiles(buf.at[slot, 0], tm) * gates[:, 0:1]
                      + _load_row_tiles(buf.at[slot, 1], tm) * gates[:, 1:2])
    o_ref[...] = _rms(y, g_ref[...]) if final else y


def _combine(x, ys, gates, dest_flat, final_g):
    T, D = x.shape
    tm = TM_MOVE
    n = T // tm
    final = final_g is not None
    g = final_g if final else jnp.ones((1, D), F32)
    tile_row = dest_flat * SUBLANES
    return pl.pallas_call(
        functools.partial(_combine_kernel, final=final),
        grid=(n,),
        in_specs=[pl.BlockSpec((TOP_K * tm,), lambda i: (i,), memory_space=pltpu.SMEM),
                  pl.BlockSpec((TOP_K * tm,), lambda i: (jnp.minimum(i + 1, n - 1),),
                               memory_space=pltpu.SMEM),
                  pl.BlockSpec((tm, D), lambda i: (i, 0)),
                  pl.BlockSpec((tm, LANES), lambda i: (i, 0)),
                  pl.BlockSpec((1, D), lambda i: (0, 0)),
                  pl.BlockSpec(memory_space=pl.ANY)],
        out_specs=pl.BlockSpec((tm, D), lambda i: (i, 0)),
        out_shape=jax.ShapeDtypeStruct((T, D), F32),
        scratch_shapes=[pltpu.VMEM((2, TOP_K, tm * SUBLANES, LANES), F32),
                        pltpu.SemaphoreType.DMA((2,))],
        compiler_params=_params("arbitrary"),
        name="moe_combine_final" if final else "moe_combine",
    )(tile_row, tile_row, x, gates, g, ys)


def _final_norm_kernel(x_ref, g_ref, o_ref):
    o_ref[...] = _rms(x_ref[...], g_ref[...])


def _final_norm(x, g):
    T, D = x.shape
    tm = TM_PROJ
    return pl.pallas_call(
        _final_norm_kernel,
        grid=(T // tm,),
        in_specs=[pl.BlockSpec((tm, D), lambda i: (i, 0)), pl.BlockSpec((1, D), lambda i: (0, 0))],
        out_specs=pl.BlockSpec((tm, D), lambda i: (i, 0)),
        out_shape=jax.ShapeDtypeStruct((T, D), F32),
        compiler_params=_params("parallel"),
        name="final_norm",
    )(x, g)


def _rope_tables(seq):
    inv = 1.0 / (ROPE_THETA ** (jnp.arange(0, MLA_ROPE, 2, dtype=F32) / MLA_ROPE))
    ang = jnp.arange(seq, dtype=F32)[:, None] * inv[None, :]
    cos, sin = jnp.cos(ang), jnp.sin(ang)
    ones = jnp.ones((seq, MLA_NOPE), F32)
    z_nope = jnp.zeros((seq, MLA_NOPE), F32)
    z_pad = jnp.zeros((seq, LANES - MLA_NOPE - MLA_ROPE), F32)
    scale = (MLA_NOPE + MLA_ROPE) ** -0.5 * np.log2(np.e)
    cq =jnp.concatenate([ones, cos, cos, z_pad], axis=1) * scale
    sq = jnp.concatenate([z_nope, sin, sin, z_pad], axis=1) * scale
    ck = jnp.concatenate([z_nope, cos, cos, z_pad], axis=1)
    sk = jnp.concatenate([z_nope, sin, sin, z_pad], axis=1)
    return cq, sq, ck, sk


def _projection_weights(attn_norm, w_in, q_norm, w_uq, kv_norm, w_ukv, w_out, ffn_norm, w_router):
    depth, d = w_in.shape[:2]
    o = np.cumsum([0, Q_LORA, KV_LORA, MLA_ROPE])
    w_cq, w_ckv, w_kr = (w_in[..., o[j]:o[j + 1]] for j in range(3))
    w_qkvb = w_in[..., o[3]:]
    x1, x2 = w_kr[..., :HALF_ROPE], w_kr[..., HALF_ROPE:]
    z_nope = jnp.zeros((depth, d, MLA_NOPE), F32)
    z_pad = jnp.zeros((depth, d, LANES - MLA_NOPE - MLA_ROPE), F32)
    w_kr2 = jnp.concatenate([z_nope, x1, x2, z_pad, z_nope, -x2, x1, z_pad], axis=-1)

    wq = w_uq.reshape(depth, Q_LORA, MLA_HEADS, MLA_NOPE + MLA_ROPE)
    nope, q1, q2 = wq[..., :MLA_NOPE], wq[..., MLA_NOPE:MLA_NOPE + HALF_ROPE], wq[..., MLA_NOPE + HALF_ROPE:]
    zq_nope = jnp.zeros((depth, Q_LORA, MLA_HEADS, MLA_NOPE), F32)
    zq_pad = jnp.zeros((depth, Q_LORA, MLA_HEADS, LANES - MLA_NOPE - MLA_ROPE), F32)
    wide = (depth, Q_LORA, MLA_HEADS * LANES)
    pre = jnp.concatenate([nope, q1, q2, zq_pad], axis=-1).reshape(wide)
    swap = jnp.concatenate([zq_nope, -q2, q1, zq_pad], axis=-1).reshape(wide)

    wkv = w_ukv.reshape(depth, KV_LORA, MLA_HEADS, MLA_NOPE + MLA_V)
    zk = jnp.zeros((depth, KV_LORA, MLA_HEADS, LANES - MLA_NOPE), F32)
    wide = (depth, KV_LORA, MLA_HEADS * LANES)
    w_uk = jnp.concatenate([wkv[..., :MLA_NOPE], zk], axis=-1).reshape(wide)
    zv = jnp.zeros((depth, KV_LORA, MLA_HEADS, LANES - MLA_V), F32)
    w_uv = jnp.concatenate([wkv[..., MLA_NOPE:], zv], axis=-1).reshape(wide)
    half = MLA_HEADS * MLA_V

    router = jnp.zeros((w_router.shape[0], d, LANES), F32).at[..., :N_EXPERTS].set(w_router)
    router_hi = router.astype(BF16)
    router_lo = (router - router_hi.astype(F32)).astype(BF16)
    return {
        'attn_norm': attn_norm[:, None], 'w_cq': w_cq.astype(BF16), 'w_ckv': w_ckv.astype(BF16),
        'w_kr': w_kr2.astype(BF16), 'w_qkvb': w_qkvb.astype(BF16), 'q_norm': q_norm[:, None],
        'w_uq': jnp.concatenate([pre, swap], axis=-1).astype(BF16), 'kv_norm': kv_norm[:, None],
        'w_uk': w_uk.astype(BF16), 'w_uv': w_uv.astype(BF16),
        'w_out_a': w_out[:, :half].astype(BF16), 'w_out_b': w_out[:, half:].astype(BF16),
        'ffn_norm': ffn_norm[:, None],
        'w_router': jnp.stack([router_hi, router_lo], axis=1),
    }


def _moe_layout(meta, counts):
    T = meta.shape[0]
    blk = TM_FFN
    cnt = counts[0, :N_EXPERTS].astype(jnp.int32)
    padded = (cnt + blk - 1) // blk * blk
    pends = jnp.cumsum(padded)
    pstarts = pends - padded
    dest = (pstarts[meta[:, :TOP_K]] + meta[:, TOP_K:2 * TOP_K]).reshape(-1).astype(jnp.int32)
    n_rows = T * TOP_K + N_EXPERTS * blk
    n_blocks = n_rows // blk
    token = jnp.arange(T * TOP_K, dtype=jnp.int32) // TOP_K
    row_token = jnp.zeros((n_rows,), jnp.int32).at[dest].set(token)
    block_expert = jnp.minimum(
        jnp.searchsorted(pends, jnp.arange(n_blocks, dtype=jnp.int32) * blk, side='right'),
        N_EXPERTS - 1).astype(jnp.int32)
    n_valid = (pends[-1:] // blk).astype(jnp.int32)
    return dest, row_token, block_expert, n_valid


def kernel(x, attn_norm, w_in, q_norm, w_uq, kv_norm, w_ukv, rel_bias, w_out, ffn_norm, dense_w1,
           dense_w3, dense_w2, w_router, moe_w1, moe_w3, moe_w2, final_norm):
    B, S, D = x.shape
    T = B * S
    depth = w_in.shape[0]
    tabs = _rope_tables(S)
    bias = _bias_tiles(_rel_bias_band(rel_bias))
    dense_w = [w.astype(BF16)[:, None] for w in (dense_w1, dense_w3, dense_w2)]
    moe_w = [w.astype(BF16) for w in (moe_w1, moe_w3, moe_w2)]
    lw = _projection_weights(attn_norm, w_in, q_norm, w_uq, kv_norm, w_ukv, w_out, ffn_norm, w_router)
    out = None
    for i in range(depth):
        q, k, v, qb, kb, vb = _pre_attn(x, lw, tabs, i)
        a = _mla_attention(q, k, v)
        b = _chunk_attention(qb, kb, vb, bias, i)
        xt = x.reshape(T, D)
        a2, b2 = a.reshape(T, -1), b.reshape(T, -1)
        j = i // 2
        last = i == depth - 1
        if i % 2 == 0:
            x2, h = _post_attn(xt, a2, b2, lw, i, route=False)
            y = _ffn(h, x2, *dense_w, j)
            if last:
                out = _final_norm(y, final_norm[None])
        else:
            x2, h, meta, gates, counts = _post_attn(xt, a2, b2, lw, i, route=True)
            dest, row_token, block_expert, n_valid = _moe_layout(meta, counts)
            ys = _moe_ffn(h, row_token, *moe_w, j, block_expert, n_valid)
            y = _combine(x2, ys, gates, dest, final_norm[None] if last else None)
            if last:
                out = y
        x = y.reshape(B, S, D)
    return out.reshape(B, S, D)
```

```python
import functools

import jax
import jax.numpy as jnp
import numpy as np
from jax import lax
from jax.experimental import pallas as pl
from jax.experimental.pallas import tpu as pltpu

D_MODEL = 1024
CHUNK = 64
MLA_HEADS = 8
MLA_NOPE = 64
MLA_ROPE = 32
MLA_V = 64
Q_LORA = 384
KV_LORA = 256
ROPE_THETA = 10000.0
CA_HEADS = 8
CA_HEAD_DIM = 64
LEFT_CHUNKS = 8
REL_CLIP = 128
FFN_DIM = 3584
N_EXPERTS = 8
TOP_K = 2
EPS = 1e-6
NEG_INF = -1e30

CA_DIM = CA_HEADS * CA_HEAD_DIM
BAND = LEFT_CHUNKS + 1
HALF_ROPE = MLA_ROPE // 2
LANES = 128
SUBLANES = 8
assert D_MODEL == SUBLANES * LANES
VMEM_LIMIT = 56 * 1024 * 1024

TM_PROJ = 512
ROUTE_CHUNK = 128
TQ_MLA = 1024
TK_MLA = 512
TQ_CA = 256
CA_WIN = LEFT_CHUNKS * CHUNK + TQ_CA
N_CA_LEAD = LEFT_CHUNKS * CHUNK // TQ_CA
TM_FFN = 512
TF_FFN = 1792
TM_DISPATCH = 512
TM_MOVE = 256
MOVE_UNROLL = 8

BF16 = jnp.bfloat16
F32 = jnp.float32


def _params(*sem):
    return pltpu.CompilerParams(dimension_semantics=sem, vmem_limit_bytes=VMEM_LIMIT)


def _rms(x, g):
    ms = jnp.mean(x * x, axis=-1, keepdims=True)
    return x * lax.rsqrt(ms + EPS) * g


def _dot(a, b):
    return jnp.dot(a, b, preferred_element_type=F32)


def _store_row_tiles(ref, value):
    n = value.shape[0]
    for s in range(SUBLANES):
        ref[pl.ds(s, n, stride=SUBLANES), :] = value[:, s * LANES:(s + 1) * LANES]


def _load_row_tiles(ref, n):
    return jnp.concatenate([ref[pl.ds(s, n, stride=SUBLANES), :] for s in range(SUBLANES)], axis=1)


def _dot_nt(a, b):
    return lax.dot_general(a, b, (((1,), (1,)), ((), ())), preferred_element_type=F32)


def _pre_attn_kernel(x_ref, an_ref, wcq_ref, wckv_ref, wkr_ref, wqkvb_ref, qn_ref, wuq_ref,
                     kvn_ref, wuk_ref, wuv_ref, cq_ref, sq_ref, ck_ref, sk_ref,
                     q_out, k_out, v_out, qb_out, kb_out, vb_out):
    h = _rms(x_ref[...], an_ref[...]).astype(BF16)
    c_q = _dot(h, wcq_ref[...])
    c_kv = _dot(h, wckv_ref[...])
    kr2 = _dot(h, wkr_ref[...])
    qkvb = _dot(h, wqkvb_ref[...])
    qb_out[...] = (qkvb[:, :CA_DIM] * (CA_HEAD_DIM ** -0.5 * np.log2(np.e))).astype(BF16)
    kb_out[...] = qkvb[:, CA_DIM:2 * CA_DIM].astype(BF16)
    vb_out[...] = qkvb[:, 2 * CA_DIM:].astype(BF16)

    q2 = _dot(_rms(c_q, qn_ref[...]).astype(BF16), wuq_ref[...])
    cq, sq = cq_ref[...], sq_ref[...]
    width = MLA_HEADS * LANES
    for hd in range(MLA_HEADS):
        lo = hd * LANES
        q_out[hd] = (q2[:, lo:lo + LANES] * cq + q2[:, width + lo:width + lo + LANES] * sq).astype(BF16)

    ckvn = _rms(c_kv, kvn_ref[...]).astype(BF16)
    kn = _dot(ckvn, wuk_ref[...])
    vv = _dot(ckvn, wuv_ref[...])
    kr = kr2[:, :LANES] * ck_ref[...] + kr2[:, LANES:] * sk_ref[...]
    lane = lax.broadcasted_iota(jnp.int32, kr.shape, 1)
    one = jnp.where(lane == MLA_V, 1.0, 0.0)
    for hd in range(MLA_HEADS):
        k_out[hd] = (kn[:, hd * LANES:(hd + 1) * LANES] + kr).astype(BF16)
        v_out[hd] = (vv[:, hd * LANES:(hd + 1) * LANES] + one).astype(BF16)


def _pre_attn(x, lw, tabs, layer):
    B, S, D = x.shape
    tm = TM_PROJ
    n_s = S // tm
    full = lambda a: pl.BlockSpec((None,) + a.shape[1:], lambda b, i: (layer,) + (0,) * (a.ndim - 1))
    tab = pl.BlockSpec((tm, LANES), lambda b, i: (i, 0))
    tok = lambda w: pl.BlockSpec((None, tm, w), lambda b, i: (b, i, 0))
    heads = lambda n: pl.BlockSpec((None, n, tm, LANES), lambda b, i: (b, 0, i, 0))
    weights = [lw['attn_norm'], lw['w_cq'], lw['w_ckv'], lw['w_kr'], lw['w_qkvb'], lw['q_norm'],
               lw['w_uq'], lw['kv_norm'], lw['w_uk'], lw['w_uv']]
    return pl.pallas_call(
        _pre_attn_kernel,
        grid=(B, n_s),
        in_specs=[tok(D)] + [full(w) for w in weights] + [tab] * 4,
        out_specs=[heads(MLA_HEADS), heads(MLA_HEADS), heads(MLA_HEADS),
                   tok(CA_DIM), tok(CA_DIM), tok(CA_DIM)],
        out_shape=[jax.ShapeDtypeStruct((B, MLA_HEADS, S, LANES), BF16),
                   jax.ShapeDtypeStruct((B, MLA_HEADS, S, LANES), BF16),
                   jax.ShapeDtypeStruct((B, MLA_HEADS, S, LANES), BF16),
                   jax.ShapeDtypeStruct((B, S, CA_DIM), BF16),
                   jax.ShapeDtypeStruct((B, S, CA_DIM), BF16),
                   jax.ShapeDtypeStruct((B, S, CA_DIM), BF16)],
        compiler_params=_params("parallel", "parallel"),
        name="pre_attn",
    )(x, *weights, *tabs)


def _mla_kernel(qi_ref, kj_ref, q_ref, k_ref, v_ref, o_ref, m_scr, acc_scr, *, tq, tk):
    t = pl.program_id(1)
    kj = kj_ref[t]
    ratio = tq // tk
    diag = kj - qi_ref[t] * ratio

    @pl.when(kj == 0)
    def _():
        m_scr[...] = jnp.full(m_scr.shape, NEG_INF, F32)
        acc_scr[...] = jnp.zeros(acc_scr.shape, F32)

    def update(hd, lo, n, visible):
        s = _dot_nt(q_ref[hd, lo:lo + n], k_ref[hd])
        if visible is not None:
            s = jnp.where(visible, s, NEG_INF)
        m_prev = m_scr[hd, lo:lo + n]
        m_new = jnp.maximum(m_prev, jnp.max(s, axis=1, keepdims=True))
        p = jnp.exp2((s - jnp.tile(m_new, (1, tk // LANES))).astype(BF16))
        m_scr[hd, lo:lo + n] = m_new
        acc_scr[hd, lo:lo + n] = (acc_scr[hd, lo:lo + n] * jnp.exp2(m_prev - m_new)
                                  + _dot(p, v_ref[hd]))

    @pl.when(diag < 0)
    def _():
        for hd in range(MLA_HEADS):
            update(hd, 0, tq, None)

    for d in range(ratio):
        @pl.when(diag == d)
        def _():
            row = lax.broadcasted_iota(jnp.int32, (tk, tk), 0) // CHUNK
            col = lax.broadcasted_iota(jnp.int32, (tk, tk), 1) // CHUNK
            visible = col <= row
            below = (d + 1) * tk
            for hd in range(MLA_HEADS):
                update(hd, d * tk, tk, visible)
                if below < tq:
                    update(hd, below, tq - below, None)

    @pl.when(diag == ratio - 1)
    def _():
        lane = lax.broadcasted_iota(jnp.int32, (tq, LANES), 1)
        for p in range(MLA_HEADS // 2):
            even, odd = acc_scr[2 * p], acc_scr[2 * p + 1]
            even = even / even[:, MLA_V:MLA_V + 1]
            odd = odd / odd[:, MLA_V:MLA_V + 1]
            pair = jnp.where(lane < MLA_V, even, pltpu.roll(odd, MLA_V, axis=1))
            o_ref[:, p * LANES:(p + 1) * LANES] = pair.astype(BF16)


def _mla_attention(q, k, v):
    B, H, S, _ = q.shape
    tq, tk = TQ_MLA, TK_MLA
    ratio = tq // tk
    pairs = [(i, j) for i in range(S // tq) for j in range(ratio * (i + 1))]
    qi = jnp.asarray(np.array([p[0] for p in pairs], np.int32))
    kj = jnp.asarray(np.array([p[1] for p in pairs], np.int32))
    kv_spec = pl.BlockSpec((None, H, tk, LANES), lambda b, t, qi, kj: (b, 0, kj[t], 0))
    grid_spec = pltpu.PrefetchScalarGridSpec(
        num_scalar_prefetch=2,
        grid=(B, len(pairs)),
        in_specs=[pl.BlockSpec((None, H, tq, LANES), lambda b, t, qi, kj: (b, 0, qi[t], 0)),
                  kv_spec, kv_spec],
        out_specs=pl.BlockSpec((None, tq, H * MLA_V), lambda b, t, qi, kj: (b, qi[t], 0)),
        scratch_shapes=[pltpu.VMEM((H, tq, LANES), F32)] * 2,
    )
    return pl.pallas_call(
        functools.partial(_mla_kernel, tq=tq, tk=tk),
        grid_spec=grid_spec,
        out_shape=jax.ShapeDtypeStruct((B, S, H * MLA_V), BF16),
        compiler_params=_params("parallel", "arbitrary"),
        name="mla_attention",
    )(qi, kj, q, k, v)


def _rel_bias_kernel(tab_ref, o_ref):
    r = lax.broadcasted_iota(jnp.int32, (CHUNK, BAND * CHUNK), 0)
    c = lax.broadcasted_iota(jnp.int32, (CHUNK, BAND * CHUNK), 1)
    rel = jnp.clip(LEFT_CHUNKS * CHUNK + r - c, -REL_CLIP, REL_CLIP) + REL_CLIP
    t_lo = REL_CLIP + max(-REL_CLIP, LEFT_CHUNKS * CHUNK - (BAND * CHUNK - 1))
    t_hi = REL_CLIP + min(REL_CLIP, LEFT_CHUNKS * CHUNK + CHUNK - 1)
    for hd in range(CA_HEADS):
        def body(t, acc):
            return jnp.where(rel == t, tab_ref[hd, t], acc)
        o_ref[hd] = lax.fori_loop(t_lo, t_hi + 1, body, jnp.zeros(rel.shape, F32))


def _rel_bias_band(rel_bias):
    depth = rel_bias.shape[0]
    return pl.pallas_call(
        _rel_bias_kernel,
        grid=(depth,),
        in_specs=[pl.BlockSpec((None, CA_HEADS, 2 * REL_CLIP + 1), lambda d: (d, 0, 0),
                               memory_space=pltpu.SMEM)],
        out_specs=pl.BlockSpec((None, CA_HEADS, CHUNK, BAND * CHUNK), lambda d: (d, 0, 0, 0)),
        out_shape=jax.ShapeDtypeStruct((depth, CA_HEADS, CHUNK, BAND * CHUNK), F32),
        compiler_params=_params("parallel"),
        name="rel_bias_band",
    )(rel_bias)


def _bias_tiles(band):
    depth = band.shape[0]
    n_qc = TQ_CA // CHUNK
    n_kc = CA_WIN // CHUNK
    neg = jnp.full((depth, CA_HEADS, CHUNK, CHUNK), NEG_INF, F32)
    rows = []
    for rc in range(n_qc):
        blocks = []
        for cc in range(n_kc):
            j = cc - rc
            blocks.append(band[..., j * CHUNK:(j + 1) * CHUNK] if 0 <= j < BAND else neg)
        rows.append(jnp.concatenate(blocks, axis=-1))
    tile = jnp.concatenate(rows, axis=-2) * np.log2(np.e)
    col = jnp.arange(CA_WIN)
    variants = [jnp.where(col >= LEFT_CHUNKS * CHUNK - i * TQ_CA, tile, NEG_INF)
                for i in range(N_CA_LEAD)] + [tile]
    return jnp.stack(variants, axis=1)


def _chunk_attn_kernel(q_ref, k2_ref, k1_ref, k0_ref, v2_ref, v1_ref, v0_ref, bias_ref, o_ref):
    tq, win = TQ_CA, CA_WIN
    lane_q = lax.broadcasted_iota(jnp.int32, (tq, LANES), 1)
    first = lane_q < CA_HEAD_DIM
    for p in range(CA_HEADS // 2):
        sl = slice(p * LANES, (p + 1) * LANES)
        q = q_ref[:, sl]
        k = jnp.concatenate([k2_ref[:, sl], k1_ref[:, sl], k0_ref[:, sl]], axis=0)
        v = jnp.concatenate([v2_ref[:, sl], v1_ref[:, sl], v0_ref[:, sl]], axis=0)
        zero = jnp.zeros_like(q)
        q2 = jnp.concatenate([jnp.where(first, q, zero), jnp.where(first, zero, q)], axis=0)
        s = _dot_nt(q2, k) + bias_ref[2 * p:2 * p + 2].reshape(2 * tq, win)
        m = jnp.max(s, axis=1, keepdims=True)
        e = jnp.exp2(s - m)
        l = jnp.sum(e, axis=1, keepdims=True)
        o = _dot(e.astype(BF16), v) / l
        o_ref[:, sl] = jnp.where(first, o[:tq], o[tq:]).astype(BF16)


def _chunk_attention(qb, kb, vb, bias, layer):
    B, S, _ = qb.shape
    tq = TQ_CA
    n = S // tq
    cur = pl.BlockSpec((None, tq, CA_DIM), lambda b, i: (b, i, 0))
    prev1 = pl.BlockSpec((None, tq, CA_DIM), lambda b, i: (b, jnp.maximum(i - 1, 0), 0))
    prev2 = pl.BlockSpec((None, tq, CA_DIM), lambda b, i: (b, jnp.maximum(i - 2, 0), 0))
    return pl.pallas_call(
        _chunk_attn_kernel,
        grid=(B, n),
        in_specs=[cur, prev2, prev1, cur, prev2, prev1, cur,
                  pl.BlockSpec((None, None) + bias.shape[2:],
                               lambda b, i: (layer, jnp.minimum(i, N_CA_LEAD), 0, 0, 0))],
        out_specs=cur,
        out_shape=jax.ShapeDtypeStruct((B, S, CA_DIM), BF16),
        compiler_params=_params("parallel", "parallel"),
        name="chunk_attention",
    )(qb, kb, kb, kb, vb, vb, vb, bias)


def _post_attn_kernel(x_ref, a_ref, b_ref, woa_ref, wob_ref, fn_ref, *rest, route):
    if route:
        wr_ref, x2_out, h_out, meta_out, gate_out, cnt_out, run_scr = rest
    else:
        x2_out, h_out = rest
    x2 = x_ref[...] + _dot(a_ref[...], woa_ref[...]) + _dot(b_ref[...], wob_ref[...])
    x2_out[...] = x2
    h = _rms(x2, fn_ref[...])
    if not route:
        h_out[...] = h.astype(h_out.dtype)
        return
    _store_row_tiles(h_out, h)

    tm = x2.shape[0]
    step = pl.program_id(0)

    @pl.when(step == 0)
    def _():
        run_scr[...] = jnp.zeros(run_scr.shape, F32)

    h_hi = h.astype(BF16)
    h_lo = (h - h_hi.astype(F32)).astype(BF16)
    all_logits = _dot(h_hi, wr_ref[0]) + (_dot(h_lo, wr_ref[0]) + _dot(h_hi, wr_ref[1]))

    rc = ROUTE_CHUNK
    lane = lax.broadcasted_iota(jnp.int32, (rc, LANES), 1)
    r_i = lax.broadcasted_iota(jnp.int32, (rc, rc), 0)
    c_i = lax.broadcasted_iota(jnp.int32, (rc, rc), 1)
    lower = jnp.where(c_i < r_i, 1.0, 0.0).astype(BF16)
    run = run_scr[...]
    for c in range(tm // rc):
        rows = slice(c * rc, (c + 1) * rc)
        logits = jnp.where(lane < N_EXPERTS, all_logits[rows], -jnp.inf)
        v1 = jnp.max(logits, axis=1, keepdims=True)
        e1 = jnp.min(jnp.where(logits == v1, lane, LANES), axis=1, keepdims=True)
        rest_logits = jnp.where(lane == e1, -jnp.inf, logits)
        v2 = jnp.max(rest_logits, axis=1, keepdims=True)
        e2 = jnp.min(jnp.where(rest_logits == v2, lane, LANES), axis=1, keepdims=True)
        w2 = jnp.exp(v2 - v1)
        g1 = 1.0 / (1.0 + w2)
        g2 = w2 / (1.0 + w2)

        sel1 = lane == e1
        sel2 = lane == e2
        cnt = jnp.where(sel1 | sel2, 1.0, 0.0)
        before = _dot(lower, cnt.astype(BF16)) + run
        rank1 = jnp.sum(jnp.where(sel1, before, 0.0), axis=1, keepdims=True)
        rank2 = jnp.sum(jnp.where(sel2, before, 0.0), axis=1, keepdims=True)
        run = run + jnp.sum(cnt, axis=0, keepdims=True)

        meta_out[rows] = jnp.where(lane == 0, e1, jnp.where(lane == 1, e2, jnp.where(
            lane == 2, rank1.astype(jnp.int32), jnp.where(lane == 3, rank2.astype(jnp.int32), 0))))
        gate_out[rows] = jnp.where(lane == 0, g1, jnp.where(lane == 1, g2, 0.0))
    run_scr[...] = run
    cnt_out[...] = run


def _post_attn(x, a, b, lw, layer, route):
    T, D = x.shape
    tm = TM_PROJ
    half = a.shape[1]
    tok = lambda w: pl.BlockSpec((tm, w), lambda i: (i, 0))
    stacked = lambda arr, l: pl.BlockSpec((None,) + arr.shape[1:], lambda i: (l,) + (0,) * (arr.ndim - 1))
    full = lambda arr: stacked(arr, layer)
    ins = [x, a, b, lw['w_out_a'], lw['w_out_b'], lw['ffn_norm']]
    in_specs = [tok(D), tok(half), tok(half), full(ins[3]), full(ins[4]), full(ins[5])]
    if route:
        h_spec = pl.BlockSpec((tm * SUBLANES, LANES), lambda i: (i, 0))
        h_shape = jax.ShapeDtypeStruct((T * SUBLANES, LANES), F32)
    else:
        h_spec, h_shape = tok(D), jax.ShapeDtypeStruct((T, D), BF16)
    out_specs = [tok(D), h_spec]
    out_shape = [jax.ShapeDtypeStruct((T, D), F32), h_shape]
    scratch = []
    if route:
        ins.append(lw['w_router'])
        in_specs.append(stacked(lw['w_router'], layer // 2))
        out_specs += [tok(LANES), tok(LANES), pl.BlockSpec((1, LANES), lambda i: (0, 0))]
        out_shape += [jax.ShapeDtypeStruct((T, LANES), jnp.int32),
                      jax.ShapeDtypeStruct((T, LANES), F32),
                      jax.ShapeDtypeStruct((1, LANES), F32)]
        scratch = [pltpu.VMEM((1, LANES), F32)]
    return pl.pallas_call(
        functools.partial(_post_attn_kernel, route=route),
        grid=(T // tm,),
        in_specs=in_specs,
        out_specs=out_specs,
        out_shape=out_shape,
        scratch_shapes=scratch,
        compiler_params=_params("arbitrary"),
        name="post_attn_route" if route else "post_attn",
    )(*ins)


def _swiglu_accumulate(h, w1_ref, w3_ref, w2_ref, o_ref):
    a = _dot(h, w1_ref[...])
    g = _dot(h, w3_ref[...])
    act = (a * jax.nn.sigmoid(a) * g).astype(BF16)
    o_ref[...] += _dot(act, w2_ref[...])


def _ffn_kernel(h_ref, res_ref, w1_ref, w3_ref, w2_ref, o_ref):
    @pl.when(pl.program_id(1) == 0)
    def _():
        o_ref[...] = res_ref[...]

    _swiglu_accumulate(h_ref[...], w1_ref, w3_ref, w2_ref, o_ref)


def _ffn(h, res, w1, w3, w2, layer):
    R, D = h.shape
    tm, tf = TM_FFN, TF_FFN
    row = pl.BlockSpec((tm, D), lambda i, f: (i, 0))
    w13 = pl.BlockSpec((None, None, D, tf), lambda i, f: (layer, 0, 0, f))
    w2s = pl.BlockSpec((None, None, tf, D), lambda i, f: (layer, 0, f, 0))
    return pl.pallas_call(
        _ffn_kernel,
        grid=(R // tm, FFN_DIM // tf),
        in_specs=[row, row, w13, w13, w2s],
        out_specs=row,
        out_shape=jax.ShapeDtypeStruct((R, D), F32),
        compiler_params=_params("parallel", "arbitrary"),
        name="swiglu_res",
    )(h, res, w1, w3, w2)


def _moe_ffn_kernel(be_ref, nv_ref, x_ref, w1_ref, w3_ref, w2_ref, o_ref, acc):
    del be_ref
    f = pl.program_id(1)

    @pl.when(f == 0)
    def _():
        acc[...] = jnp.zeros(acc.shape, F32)

    @pl.when(pl.program_id(0) < nv_ref[0])
    def _():
        h = _load_row_tiles(x_ref, TM_FFN).astype(BF16)
        _swiglu_accumulate(h, w1_ref, w3_ref, w2_ref, acc)

    @pl.when(f == pl.num_programs(1) - 1)
    def _():
        _store_row_tiles(o_ref, acc[...])


def _moe_ffn(xs, w1, w3, w2, layer, block_expert, n_valid):
    R = xs.shape[0] // SUBLANES
    D = D_MODEL
    tm, tf = TM_FFN, TF_FFN
    n_f = FFN_DIM // tf
    last = lambda i, nv: jnp.minimum(i, nv[0] - 1)
    fcl = lambda i, f, nv: jnp.where(i < nv[0], f, n_f - 1)
    w13 = pl.BlockSpec((None, None, D, tf),
                       lambda i, f, be, nv: (layer, be[last(i, nv)], 0, fcl(i, f, nv)))
    w2s = pl.BlockSpec((None, None, tf, D),
                       lambda i, f, be, nv: (layer, be[last(i, nv)], fcl(i, f, nv), 0))
    rows = pl.BlockSpec((tm * SUBLANES, LANES), lambda i, f, be, nv: (last(i, nv), 0))
    return pl.pallas_call(
        _moe_ffn_kernel,
        grid_spec=pltpu.PrefetchScalarGridSpec(
            num_scalar_prefetch=2, grid=(R // tm, n_f),
            in_specs=[rows, w13, w13, w2s],
            out_specs=pl.BlockSpec((tm * SUBLANES, LANES), lambda i, f, be, nv: (i, 0)),
            scratch_shapes=[pltpu.VMEM((tm, D), F32)]),
        out_shape=jax.ShapeDtypeStruct((R * SUBLANES, LANES), F32),
        compiler_params=_params("parallel", "arbitrary"),
        name="swiglu_expert",
    )(block_expert, n_valid, xs, w1, w3, w2)


def _start_rows(copy, n):
    def body(r, c):
        for k in range(TOP_K):
            copy(r, k).start(priority=k)
        return c
    lax.fori_loop(0, n, body, 0, unroll=MOVE_UNROLL)


def _wait_rows(copy, n):
    def body(r, c):
        for k in range(TOP_K):
            copy(r, k).wait()
        return c
    lax.fori_loop(0, n, body, 0, unroll=MOVE_UNROLL)


def _dispatch_kernel(pend_ref, dest_ref, h_ref, xs_out, zbuf, sem, zsem):
    blk = TM_FFN * SUBLANES

    @pl.when(pl.program_id(0) == 0)
    def _():
        zbuf[...] = jnp.zeros(zbuf.shape, F32)

        def zero_block(first_row):
            start = pl.multiple_of(first_row * SUBLANES, SUBLANES)
            zero = pltpu.make_async_copy(zbuf, xs_out.at[pl.ds(start, blk)], zsem)
            zero.start()
            zero.wait()

        for e in range(N_EXPERTS):
            @pl.when(pend_ref[e] > 0)
            def _():
                zero_block(pend_ref[e] - TM_FFN)

            unused = pend_ref[N_EXPERTS - 1] + e * TM_FFN

            @pl.when(unused * SUBLANES < xs_out.shape[0])
            def _():
                zero_block(unused)

    def copy(r, k):
        src = h_ref.at[pl.ds(pl.multiple_of(r * SUBLANES, SUBLANES), SUBLANES)]
        dst = xs_out.at[pl.ds(pl.multiple_of(dest_ref[TOP_K * r + k], SUBLANES), SUBLANES)]
        return pltpu.make_async_copy(src, dst, sem)

    _start_rows(copy, TM_DISPATCH)
    _wait_rows(copy, TM_DISPATCH)


def _dispatch(h, dest_flat, pends, n_rows):
    tm = TM_DISPATCH
    T = h.shape[0] // SUBLANES
    return pl.pallas_call(
        _dispatch_kernel,
        grid_spec=pltpu.PrefetchScalarGridSpec(
            num_scalar_prefetch=1, grid=(T // tm,),
            in_specs=[pl.BlockSpec((TOP_K * tm,), lambda i, pe: (i,), memory_space=pltpu.SMEM),
                      pl.BlockSpec((tm * SUBLANES, LANES), lambda i, pe: (i, 0))],
            out_specs=pl.BlockSpec(memory_space=pl.ANY),
            scratch_shapes=[pltpu.VMEM((TM_FFN * SUBLANES, LANES), F32),
                            pltpu.SemaphoreType.DMA(()), pltpu.SemaphoreType.DMA(())]),
        out_shape=jax.ShapeDtypeStruct((n_rows * SUBLANES, LANES), F32),
        compiler_params=_params("arbitrary"),
        name="moe_dispatch",
    )(pends, dest_flat * SUBLANES, h)


def _combine_kernel(dest_ref, next_ref, x_ref, gate_ref, g_ref, ys_hbm, o_ref, buf, sems, *, final):
    tm = TM_MOVE
    i = pl.program_id(0)
    slot = i % 2

    def copies(idx_ref, s):
        def copy(r, k):
            src = ys_hbm.at[pl.ds(pl.multiple_of(idx_ref[TOP_K * r + k], SUBLANES), SUBLANES)]
            dst = buf.at[s, k, pl.ds(pl.multiple_of(r * SUBLANES, SUBLANES), SUBLANES)]
            return pltpu.make_async_copy(src, dst, sems.at[s])
        return copy

    @pl.when(i == 0)
    def _():
        _start_rows(copies(dest_ref, slot), tm)

    @pl.when(i + 1 < pl.num_programs(0))
    def _():
        _start_rows(copies(next_ref, 1 - slot), tm)

    _wait_rows(copies(dest_ref, slot), tm)
    gates = gate_ref[...]
    y = x_ref[...] + (_load_row_tiles(buf.at[slot, 0], tm) * gates[:, 0:1]
                      + _load_row_tiles(buf.at[slot, 1], tm) * gates[:, 1:2])
    o_ref[...] = _rms(y, g_ref[...]) if final else y


def _combine(x, ys, gates, dest_flat, final_g):
    T, D = x.shape
    tm = TM_MOVE
    n = T // tm
    final = final_g is not None
    g = final_g if final else jnp.ones((1, D), F32)
    tile_row = dest_flat * SUBLANES
    return pl.pallas_call(
        functools.partial(_combine_kernel, final=final),
        grid=(n,),
        in_specs=[pl.BlockSpec((TOP_K * tm,), lambda i: (i,), memory_space=pltpu.SMEM),
                  pl.BlockSpec((TOP_K * tm,), lambda i: (jnp.minimum(i + 1, n - 1),),
                               memory_space=pltpu.SMEM),
                  pl.BlockSpec((tm, D), lambda i: (i, 0)),
                  pl.BlockSpec((tm, LANES), lambda i: (i, 0)),
                  pl.BlockSpec((1, D), lambda i: (0, 0)),
                  pl.BlockSpec(memory_space=pl.ANY)],
        out_specs=pl.BlockSpec((tm, D), lambda i: (i, 0)),
        out_shape=jax.ShapeDtypeStruct((T, D), F32),
        scratch_shapes=[pltpu.VMEM((2, TOP_K, tm * SUBLANES, LANES), F32),
                        pltpu.SemaphoreType.DMA((2,))],
        compiler_params=_params("arbitrary"),
        name="moe_combine_final" if final else "moe_combine",
    )(tile_row, tile_row, x, gates, g, ys)


def _final_norm_kernel(x_ref, g_ref, o_ref):
    o_ref[...] = _rms(x_ref[...], g_ref[...])


def _final_norm(x, g):
    T, D = x.shape
    tm = TM_PROJ
    return pl.pallas_call(
        _final_norm_kernel,
        grid=(T // tm,),
        in_specs=[pl.BlockSpec((tm, D), lambda i: (i, 0)), pl.BlockSpec((1, D), lambda i: (0, 0))],
        out_specs=pl.BlockSpec((tm, D), lambda i: (i, 0)),
        out_shape=jax.ShapeDtypeStruct((T, D), F32),
        compiler_params=_params("parallel"),
        name="final_norm",
    )(x, g)


def _rope_tables(seq):
    inv = 1.0 / (ROPE_THETA ** (jnp.arange(0, MLA_ROPE, 2, dtype=F32) / MLA_ROPE))
    ang = jnp.arange(seq, dtype=F32)[:, None] * inv[None, :]
    cos, sin = jnp.cos(ang), jnp.sin(ang)
    ones = jnp.ones((seq, MLA_NOPE), F32)
    z_nope = jnp.zeros((seq, MLA_NOPE), F32)
    z_pad = jnp.zeros((seq, LANES - MLA_NOPE - MLA_ROPE), F32)
    scale = (MLA_NOPE + MLA_ROPE) ** -0.5 * np.log2(np.e)
    cq =jnp.concatenate([ones, cos, cos, z_pad], axis=1) * scale
    sq = jnp.concatenate([z_nope, sin, sin, z_pad], axis=1) * scale
    ck = jnp.concatenate([z_nope, cos, cos, z_pad], axis=1)
    sk = jnp.concatenate([z_nope, sin, sin, z_pad], axis=1)
    return cq, sq, ck, sk


def _projection_weights(attn_norm, w_in, q_norm, w_uq, kv_norm, w_ukv, w_out, ffn_norm, w_router):
    depth, d = w_in.shape[:2]
    o = np.cumsum([0, Q_LORA, KV_LORA, MLA_ROPE])
    w_cq, w_ckv, w_kr = (w_in[..., o[j]:o[j + 1]] for j in range(3))
    w_qkvb = w_in[..., o[3]:]
    x1, x2 = w_kr[..., :HALF_ROPE], w_kr[..., HALF_ROPE:]
    z_nope = jnp.zeros((depth, d, MLA_NOPE), F32)
    z_pad = jnp.zeros((depth, d, LANES - MLA_NOPE - MLA_ROPE), F32)
    w_kr2 = jnp.concatenate([z_nope, x1, x2, z_pad, z_nope, -x2, x1, z_pad], axis=-1)

    wq = w_uq.reshape(depth, Q_LORA, MLA_HEADS, MLA_NOPE + MLA_ROPE)
    nope, q1, q2 = wq[..., :MLA_NOPE], wq[..., MLA_NOPE:MLA_NOPE + HALF_ROPE], wq[..., MLA_NOPE + HALF_ROPE:]
    zq_nope = jnp.zeros((depth, Q_LORA, MLA_HEADS, MLA_NOPE), F32)
    zq_pad = jnp.zeros((depth, Q_LORA, MLA_HEADS, LANES - MLA_NOPE - MLA_ROPE), F32)
    wide = (depth, Q_LORA, MLA_HEADS * LANES)
    pre = jnp.concatenate([nope, q1, q2, zq_pad], axis=-1).reshape(wide)
    swap = jnp.concatenate([zq_nope, -q2, q1, zq_pad], axis=-1).reshape(wide)

    wkv = w_ukv.reshape(depth, KV_LORA, MLA_HEADS, MLA_NOPE + MLA_V)
    zk = jnp.zeros((depth, KV_LORA, MLA_HEADS, LANES - MLA_NOPE), F32)
    wide = (depth, KV_LORA, MLA_HEADS * LANES)
    w_uk = jnp.concatenate([wkv[..., :MLA_NOPE], zk], axis=-1).reshape(wide)
    zv = jnp.zeros((depth, KV_LORA, MLA_HEADS, LANES - MLA_V), F32)
    w_uv = jnp.concatenate([wkv[..., MLA_NOPE:], zv], axis=-1).reshape(wide)
    half = MLA_HEADS * MLA_V

    router = jnp.zeros((w_router.shape[0], d, LANES), F32).at[..., :N_EXPERTS].set(w_router)
    router_hi = router.astype(BF16)
    router_lo = (router - router_hi.astype(F32)).astype(BF16)
    return {
        'attn_norm': attn_norm[:, None], 'w_cq': w_cq.astype(BF16), 'w_ckv': w_ckv.astype(BF16),
        'w_kr': w_kr2.astype(BF16), 'w_qkvb': w_qkvb.astype(BF16), 'q_norm': q_norm[:, None],
        'w_uq': jnp.concatenate([pre, swap], axis=-1).astype(BF16), 'kv_norm': kv_norm[:, None],
        'w_uk': w_uk.astype(BF16), 'w_uv': w_uv.astype(BF16),
        'w_out_a': w_out[:, :half].astype(BF16), 'w_out_b': w_out[:, half:].astype(BF16),
        'ffn_norm': ffn_norm[:, None],
        'w_router': jnp.stack([router_hi, router_lo], axis=1),
    }


def _moe_layout(meta, counts):
    T = meta.shape[0]
    blk = TM_FFN
    cnt = counts[0, :N_EXPERTS].astype(jnp.int32)
    padded = (cnt + blk - 1) // blk * blk
    pends = jnp.cumsum(padded).astype(jnp.int32)
    pstarts = pends - padded
    dest = (pstarts[meta[:, :TOP_K]] + meta[:, TOP_K:2 * TOP_K]).reshape(-1).astype(jnp.int32)
    n_rows = T * TOP_K + N_EXPERTS * blk
    n_blocks = n_rows // blk
    block_expert = jnp.minimum(
        jnp.searchsorted(pends, jnp.arange(n_blocks, dtype=jnp.int32) * blk, side='right'),
        N_EXPERTS - 1).astype(jnp.int32)
    n_valid = pends[-1:] // blk
    return dest, pends, block_expert, n_valid, n_rows


def kernel(x, attn_norm, w_in, q_norm, w_uq, kv_norm, w_ukv, rel_bias, w_out, ffn_norm, dense_w1,
           dense_w3, dense_w2, w_router, moe_w1, moe_w3, moe_w2, final_norm):
    B, S, D = x.shape
    T = B * S
    depth = w_in.shape[0]
    tabs = _rope_tables(S)
    bias = _bias_tiles(_rel_bias_band(rel_bias))
    dense_w = [w.astype(BF16)[:, None] for w in (dense_w1, dense_w3, dense_w2)]
    moe_w = [w.astype(BF16) for w in (moe_w1, moe_w3, moe_w2)]
    lw = _projection_weights(attn_norm, w_in, q_norm, w_uq, kv_norm, w_ukv, w_out, ffn_norm, w_router)
    out = None
    for i in range(depth):
        q, k, v, qb, kb, vb = _pre_attn(x, lw, tabs, i)
        a = _mla_attention(q, k, v)
        b = _chunk_attention(qb, kb, vb, bias, i)
        xt = x.reshape(T, D)
        a2, b2 = a.reshape(T, -1), b.reshape(T, -1)
        j = i // 2
        last = i == depth - 1
        if i % 2 == 0:
            x2, h = _post_attn(xt, a2, b2, lw, i, route=False)
            y = _ffn(h, x2, *dense_w, j)
            if last:
                out = _final_norm(y, final_norm[None])
        else:
            x2, h, meta, gates, counts = _post_attn(xt, a2, b2, lw, i, route=True)
            dest, pends, block_expert, n_valid, n_rows = _moe_layout(meta, counts)
            xs = _dispatch(h, dest, pends, n_rows)
            ys = _moe_ffn(xs, *moe_w, j, block_expert, n_valid)
            y = _combine(x2, ys, gates, dest, final_norm[None] if last else None)
            if last:
                out = y
        x = y.reshape(B, S, D)
    return out.reshape(B, S, D)
```

```python
import functools

import jax
import jax.numpy as jnp
import numpy as np
from jax import lax
from jax.experimental import pallas as pl
from jax.experimental.pallas import tpu as pltpu

D_MODEL = 1024
CHUNK = 64
MLA_HEADS = 8
MLA_NOPE = 64
MLA_ROPE = 32
MLA_V = 64
Q_LORA = 384
KV_LORA = 256
ROPE_THETA = 10000.0
CA_HEADS = 8
CA_HEAD_DIM = 64
LEFT_CHUNKS = 8
REL_CLIP = 128
FFN_DIM = 3584
N_EXPERTS = 8
TOP_K = 2
EPS = 1e-6
NEG_INF = -1e30

CA_DIM = CA_HEADS * CA_HEAD_DIM
BAND = LEFT_CHUNKS + 1
HALF_ROPE = MLA_ROPE // 2
LANES = 128
SUBLANES = 8
assert D_MODEL == SUBLANES * LANES
VMEM_LIMIT = 56 * 1024 * 1024

TM_PROJ = 512
ROUTE_CHUNK = 128
TQ_MLA = 1024
TK_MLA = 512
TQ_CA = 256
CA_WIN = LEFT_CHUNKS * CHUNK + TQ_CA
N_CA_LEAD = LEFT_CHUNKS * CHUNK // TQ_CA
TM_FFN = 512
TF_FFN = 1792
TM_DISPATCH = 512
TM_MOVE = 256
MOVE_UNROLL = 8

BF16 = jnp.bfloat16
F32 = jnp.float32


def _params(*sem):
    return pltpu.CompilerParams(dimension_semantics=sem, vmem_limit_bytes=VMEM_LIMIT)


def _rms(x, g):
    ms = jnp.mean(x * x, axis=-1, keepdims=True)
    return x * lax.rsqrt(ms + EPS) * g


def _dot(a, b):
    return jnp.dot(a, b, preferred_element_type=F32)


def _store_row_tiles(ref, value):
    n = value.shape[0]
    for s in range(SUBLANES):
        ref[pl.ds(s, n, stride=SUBLANES), :] = value[:, s * LANES:(s + 1) * LANES]


def _load_row_tiles(ref, n):
    return jnp.concatenate([ref[pl.ds(s, n, stride=SUBLANES), :] for s in range(SUBLANES)], axis=1)


def _dot_nt(a, b):
    return lax.dot_general(a, b, (((1,), (1,)), ((), ())), preferred_element_type=F32)


def _rotary(x, cos_t, sin_up, sin_dn):
    up = pltpu.roll(x, HALF_ROPE, axis=1)
    down = pltpu.roll(x, LANES - HALF_ROPE, axis=1)
    return x * cos_t + up * sin_up + down * sin_dn


def _pre_attn_kernel(x_ref, an_ref, wcq_ref, wckv_ref, wkr_ref, wqkvb_ref, qn_ref, wuq_ref,
                     kvn_ref, wuk_ref, wuv_ref, cq_ref, squ_ref, sqd_ref, ck_ref, sku_ref, skd_ref,
                     q_out, k_out, v_out, qb_out, kb_out, vb_out):
    h = _rms(x_ref[...], an_ref[...]).astype(BF16)
    c_q = _dot(h, wcq_ref[...])
    c_kv = _dot(h, wckv_ref[...])
    k_r = _dot(h, wkr_ref[...])
    qkvb = _dot(h, wqkvb_ref[...])
    qb_out[...] = (qkvb[:, :CA_DIM] * (CA_HEAD_DIM ** -0.5 * np.log2(np.e))).astype(BF16)
    kb_out[...] = qkvb[:, CA_DIM:2 * CA_DIM].astype(BF16)
    vb_out[...] = qkvb[:, 2 * CA_DIM:].astype(BF16)

    qp = _dot(_rms(c_q, qn_ref[...]).astype(BF16), wuq_ref[...])
    cq, squ, sqd = cq_ref[...], squ_ref[...], sqd_ref[...]
    for hd in range(MLA_HEADS):
        q_out[hd] = _rotary(qp[:, hd * LANES:(hd + 1) * LANES], cq, squ, sqd).astype(BF16)

    ckvn = _rms(c_kv, kvn_ref[...]).astype(BF16)
    kn = _dot(ckvn, wuk_ref[...])
    vv = _dot(ckvn, wuv_ref[...])
    kr = _rotary(k_r, ck_ref[...], sku_ref[...], skd_ref[...])
    lane = lax.broadcasted_iota(jnp.int32, kr.shape, 1)
    one = jnp.where(lane == MLA_V, 1.0, 0.0)
    for hd in range(MLA_HEADS):
        k_out[hd] = (kn[:, hd * LANES:(hd + 1) * LANES] + kr).astype(BF16)
        v_out[hd] = (vv[:, hd * LANES:(hd + 1) * LANES] + one).astype(BF16)


def _pre_attn(x, lw, tabs, layer):
    B, S, D = x.shape
    tm = TM_PROJ
    n_s = S // tm
    full = lambda a: pl.BlockSpec((None,) + a.shape[1:], lambda b, i: (layer,) + (0,) * (a.ndim - 1))
    tab = pl.BlockSpec((tm, LANES), lambda b, i: (i, 0))
    tok = lambda w: pl.BlockSpec((None, tm, w), lambda b, i: (b, i, 0))
    heads = lambda n: pl.BlockSpec((None, n, tm, LANES), lambda b, i: (b, 0, i, 0))
    weights = [lw['attn_norm'], lw['w_cq'], lw['w_ckv'], lw['w_kr'], lw['w_qkvb'], lw['q_norm'],
               lw['w_uq'], lw['kv_norm'], lw['w_uk'], lw['w_uv']]
    return pl.pallas_call(
        _pre_attn_kernel,
        grid=(B, n_s),
        in_specs=[tok(D)] + [full(w) for w in weights] + [tab] * len(tabs),
        out_specs=[heads(MLA_HEADS), heads(MLA_HEADS), heads(MLA_HEADS),
                   tok(CA_DIM), tok(CA_DIM), tok(CA_DIM)],
        out_shape=[jax.ShapeDtypeStruct((B, MLA_HEADS, S, LANES), BF16),
                   jax.ShapeDtypeStruct((B, MLA_HEADS, S, LANES), BF16),
                   jax.ShapeDtypeStruct((B, MLA_HEADS, S, LANES), BF16),
                   jax.ShapeDtypeStruct((B, S, CA_DIM), BF16),
                   jax.ShapeDtypeStruct((B, S, CA_DIM), BF16),
                   jax.ShapeDtypeStruct((B, S, CA_DIM), BF16)],
        compiler_params=_params("parallel", "parallel"),
        name="pre_attn",
    )(x, *weights, *tabs)


def _mla_kernel(qi_ref, kj_ref, q_ref, k_ref, v_ref, o_ref, m_scr, acc_scr, *, tq, tk):
    t = pl.program_id(1)
    kj = kj_ref[t]
    ratio = tq // tk
    diag = kj - qi_ref[t] * ratio

    @pl.when(kj == 0)
    def _():
        m_scr[...] = jnp.full(m_scr.shape, NEG_INF, F32)
        acc_scr[...] = jnp.zeros(acc_scr.shape, F32)

    def update(hd, lo, n, visible):
        s = _dot_nt(q_ref[hd, lo:lo + n], k_ref[hd])
        if visible is not None:
            s = jnp.where(visible, s, NEG_INF)
        m_prev = m_scr[hd, lo:lo + n]
        m_new = jnp.maximum(m_prev, jnp.max(s, axis=1, keepdims=True))
        p = jnp.exp2((s - jnp.tile(m_new, (1, tk // LANES))).astype(BF16))
        m_scr[hd, lo:lo + n] = m_new
        acc_scr[hd, lo:lo + n] = (acc_scr[hd, lo:lo + n] * jnp.exp2(m_prev - m_new)
                                  + _dot(p, v_ref[hd]))

    @pl.when(diag < 0)
    def _():
        for hd in range(MLA_HEADS):
            update(hd, 0, tq, None)

    for d in range(ratio):
        @pl.when(diag == d)
        def _():
            n = tq - d * tk
            row = lax.broadcasted_iota(jnp.int32, (n, tk), 0) // CHUNK
            col = lax.broadcasted_iota(jnp.int32, (n, tk), 1) // CHUNK
            visible = col <= row
            for hd in range(MLA_HEADS):
                update(hd, d * tk, n, visible)

    @pl.when(diag == ratio - 1)
    def _():
        lane = lax.broadcasted_iota(jnp.int32, (tq, LANES), 1)

        def normalised(acc):
            denom = jnp.sum(jnp.where(lane == MLA_V, acc, 0.0), axis=1, keepdims=True)
            return acc * (1.0 / denom)

        for p in range(MLA_HEADS // 2):
            even, odd = normalised(acc_scr[2 * p]), normalised(acc_scr[2 * p + 1])
            pair = jnp.where(lane < MLA_V, even, pltpu.roll(odd, MLA_V, axis=1))
            o_ref[:, p * LANES:(p + 1) * LANES] = pair.astype(BF16)


def _mla_attention(q, k, v):
    B, H, S, _ = q.shape
    tq, tk = TQ_MLA, TK_MLA
    ratio = tq // tk
    pairs = [(i, j) for i in range(S // tq) for j in range(ratio * (i + 1))]
    qi = jnp.asarray(np.array([p[0] for p in pairs], np.int32))
    kj = jnp.asarray(np.array([p[1] for p in pairs], np.int32))
    kv_spec = pl.BlockSpec((None, H, tk, LANES), lambda b, t, qi, kj: (b, 0, kj[t], 0))
    grid_spec = pltpu.PrefetchScalarGridSpec(
        num_scalar_prefetch=2,
        grid=(B, len(pairs)),
        in_specs=[pl.BlockSpec((None, H, tq, LANES), lambda b, t, qi, kj: (b, 0, qi[t], 0)),
                  kv_spec, kv_spec],
        out_specs=pl.BlockSpec((None, tq, H * MLA_V), lambda b, t, qi, kj: (b, qi[t], 0)),
        scratch_shapes=[pltpu.VMEM((H, tq, LANES), F32)] * 2,
    )
    return pl.pallas_call(
        functools.partial(_mla_kernel, tq=tq, tk=tk),
        grid_spec=grid_spec,
        out_shape=jax.ShapeDtypeStruct((B, S, H * MLA_V), BF16),
        compiler_params=_params("parallel", "arbitrary"),
        name="mla_attention",
    )(qi, kj, q, k, v)


def _rel_bias_kernel(tab_ref, o_ref):
    far = max(0, (LEFT_CHUNKS * CHUNK - REL_CLIP) // LANES * LANES)
    near = BAND * CHUNK - far
    r = lax.broadcasted_iota(jnp.int32, (CHUNK, near), 0)
    c = lax.broadcasted_iota(jnp.int32, (CHUNK, near), 1) + far
    rel = jnp.clip(LEFT_CHUNKS * CHUNK + r - c, -REL_CLIP, REL_CLIP) + REL_CLIP
    t_lo = REL_CLIP + max(-REL_CLIP, LEFT_CHUNKS * CHUNK - (BAND * CHUNK - 1))
    t_hi = REL_CLIP + min(REL_CLIP, LEFT_CHUNKS * CHUNK + CHUNK - 1)
    for hd in range(CA_HEADS):
        def body(t, acc):
            return jnp.where(rel == t, tab_ref[hd, t], acc)
        o_ref[hd, :, far:] = lax.fori_loop(t_lo, t_hi + 1, body, jnp.zeros(rel.shape, F32))
        if far:
            o_ref[hd, :, :far] = jnp.full((CHUNK, far), tab_ref[hd, 2 * REL_CLIP], F32)


def _rel_bias_band(rel_bias):
    depth = rel_bias.shape[0]
    return pl.pallas_call(
        _rel_bias_kernel,
        grid=(depth,),
        in_specs=[pl.BlockSpec((None, CA_HEADS, 2 * REL_CLIP + 1), lambda d: (d, 0, 0),
                               memory_space=pltpu.SMEM)],
        out_specs=pl.BlockSpec((None, CA_HEADS, CHUNK, BAND * CHUNK), lambda d: (d, 0, 0, 0)),
        out_shape=jax.ShapeDtypeStruct((depth, CA_HEADS, CHUNK, BAND * CHUNK), F32),
        compiler_params=_params("parallel"),
        name="rel_bias_band",
    )(rel_bias)


def _bias_tiles(band):
    depth = band.shape[0]
    n_qc = TQ_CA // CHUNK
    n_kc = CA_WIN // CHUNK
    neg = jnp.full((depth, CA_HEADS, CHUNK, CHUNK), NEG_INF, F32)
    rows = []
    for rc in range(n_qc):
        blocks = []
        for cc in range(n_kc):
            j = cc - rc
            blocks.append(band[..., j * CHUNK:(j + 1) * CHUNK] if 0 <= j < BAND else neg)
        rows.append(jnp.concatenate(blocks, axis=-1))
    tile = jnp.concatenate(rows, axis=-2) * np.log2(np.e)
    col = jnp.arange(CA_WIN)
    variants = [jnp.where(col >= LEFT_CHUNKS * CHUNK - i * TQ_CA, tile, NEG_INF)
                for i in range(N_CA_LEAD)] + [tile]
    return jnp.stack(variants, axis=1)


def _chunk_attn_kernel(q_ref, k2_ref, k1_ref, k0_ref, v2_ref, v1_ref, v0_ref, bias_ref, o_ref):
    tq, win = TQ_CA, CA_WIN
    lane_q = lax.broadcasted_iota(jnp.int32, (tq, LANES), 1)
    first = lane_q < CA_HEAD_DIM
    for p in range(CA_HEADS // 2):
        sl = slice(p * LANES, (p + 1) * LANES)
        q = q_ref[:, sl]
        k = jnp.concatenate([k2_ref[:, sl], k1_ref[:, sl], k0_ref[:, sl]], axis=0)
        v = jnp.concatenate([v2_ref[:, sl], v1_ref[:, sl], v0_ref[:, sl]], axis=0)
        zero = jnp.zeros_like(q)
        q2 = jnp.concatenate([jnp.where(first, q, zero), jnp.where(first, zero, q)], axis=0)
        s = _dot_nt(q2, k) + bias_ref[2 * p:2 * p + 2].reshape(2 * tq, win)
        m = jnp.max(s, axis=1, keepdims=True)
        e = jnp.exp2(s - m)
        l = jnp.sum(e, axis=1, keepdims=True)
        o = _dot(e.astype(BF16), v) / l
        o_ref[:, sl] = jnp.where(first, o[:tq], o[tq:]).astype(BF16)


def _chunk_attention(qb, kb, vb, bias, layer):
    B, S, _ = qb.shape
    tq = TQ_CA
    n = S // tq
    cur = pl.BlockSpec((None, tq, CA_DIM), lambda b, i: (b, i, 0))
    prev1 = pl.BlockSpec((None, tq, CA_DIM), lambda b, i: (b, jnp.maximum(i - 1, 0), 0))
    prev2 = pl.BlockSpec((None, tq, CA_DIM), lambda b, i: (b, jnp.maximum(i - 2, 0), 0))
    return pl.pallas_call(
        _chunk_attn_kernel,
        grid=(B, n),
        in_specs=[cur, prev2, prev1, cur, prev2, prev1, cur,
                  pl.BlockSpec((None, None) + bias.shape[2:],
                               lambda b, i: (layer, jnp.minimum(i, N_CA_LEAD), 0, 0, 0))],
        out_specs=cur,
        out_shape=jax.ShapeDtypeStruct((B, S, CA_DIM), BF16),
        compiler_params=_params("parallel", "parallel"),
        name="chunk_attention",
    )(qb, kb, kb, kb, vb, vb, vb, bias)


def _post_attn_kernel(x_ref, a_ref, b_ref, woa_ref, wob_ref, fn_ref, *rest, route):
    if route:
        wr_ref, x2_out, h_out, meta_out, gate_out, cnt_out, run_scr = rest
    else:
        x2_out, h_out = rest
    ab = jnp.concatenate([a_ref[...], b_ref[...]], axis=1)
    wo = jnp.concatenate([woa_ref[...], wob_ref[...]], axis=0)
    x2 = x_ref[...] + _dot(ab, wo)
    x2_out[...] = x2
    h = _rms(x2, fn_ref[...])
    if not route:
        h_out[...] = h.astype(h_out.dtype)
        return
    _store_row_tiles(h_out, h)

    tm = x2.shape[0]
    step = pl.program_id(0)

    @pl.when(step == 0)
    def _():
        run_scr[...] = jnp.zeros(run_scr.shape, F32)

    h_hi = h.astype(BF16)
    h_lo = (h - h_hi.astype(F32)).astype(BF16)
    all_logits = _dot(jnp.concatenate([h_hi, h_lo, h_hi], axis=1),
                      jnp.concatenate([wr_ref[0], wr_ref[0], wr_ref[1]], axis=0))

    rc = ROUTE_CHUNK
    lane = lax.broadcasted_iota(jnp.int32, (rc, LANES), 1)
    r_i = lax.broadcasted_iota(jnp.int32, (rc, rc), 0)
    c_i = lax.broadcasted_iota(jnp.int32, (rc, rc), 1)
    lower = jnp.where(c_i < r_i, 1.0, 0.0).astype(BF16)
    run = run_scr[...]
    for c in range(tm // rc):
        rows = slice(c * rc, (c + 1) * rc)
        logits = jnp.where(lane < N_EXPERTS, all_logits[rows], -jnp.inf)
        v1 = jnp.max(logits, axis=1, keepdims=True)
        e1 = jnp.min(jnp.where(logits == v1, lane, LANES), axis=1, keepdims=True)
        rest_logits = jnp.where(lane == e1, -jnp.inf, logits)
        v2 = jnp.max(rest_logits, axis=1, keepdims=True)
        e2 = jnp.min(jnp.where(rest_logits == v2, lane, LANES), axis=1, keepdims=True)
        w2 = jnp.exp(v2 - v1)
        g1 = 1.0 / (1.0 + w2)
        g2 = w2 / (1.0 + w2)

        sel1 = lane == e1
        sel2 = lane == e2
        cnt = jnp.where(sel1 | sel2, 1.0, 0.0)
        before = _dot(lower, cnt.astype(BF16)) + run
        rank1 = jnp.sum(jnp.where(sel1, before, 0.0), axis=1, keepdims=True)
        rank2 = jnp.sum(jnp.where(sel2, before, 0.0), axis=1, keepdims=True)
        run = run + jnp.sum(cnt, axis=0, keepdims=True)

        meta_out[rows] = jnp.where(lane == 0, e1, jnp.where(lane == 1, e2, jnp.where(
            lane == 2, rank1.astype(jnp.int32), jnp.where(lane == 3, rank2.astype(jnp.int32), 0))))
        gate_out[rows] = jnp.where(lane == 0, g1, jnp.where(lane == 1, g2, 0.0))
    run_scr[...] = run
    cnt_out[...] = run


def _post_attn(x, a, b, lw, layer, route):
    T, D = x.shape
    tm = TM_PROJ
    half = a.shape[1]
    tok = lambda w: pl.BlockSpec((tm, w), lambda i: (i, 0))
    stacked = lambda arr, l: pl.BlockSpec((None,) + arr.shape[1:], lambda i: (l,) + (0,) * (arr.ndim - 1))
    full = lambda arr: stacked(arr, layer)
    ins = [x, a, b, lw['w_out_a'], lw['w_out_b'], lw['ffn_norm']]
    in_specs = [tok(D), tok(half), tok(half), full(ins[3]), full(ins[4]), full(ins[5])]
    if route:
        h_spec = pl.BlockSpec((tm * SUBLANES, LANES), lambda i: (i, 0))
        h_shape = jax.ShapeDtypeStruct((T * SUBLANES, LANES), F32)
    else:
        h_spec, h_shape = tok(D), jax.ShapeDtypeStruct((T, D), BF16)
    out_specs = [tok(D), h_spec]
    out_shape = [jax.ShapeDtypeStruct((T, D), F32), h_shape]
    scratch = []
    if route:
        ins.append(lw['w_router'])
        in_specs.append(stacked(lw['w_router'], layer // 2))
        out_specs += [tok(LANES), tok(LANES), pl.BlockSpec((1, LANES), lambda i: (0, 0))]
        out_shape += [jax.ShapeDtypeStruct((T, LANES), jnp.int32),
                      jax.ShapeDtypeStruct((T, LANES), F32),
                      jax.ShapeDtypeStruct((1, LANES), F32)]
        scratch = [pltpu.VMEM((1, LANES), F32)]
    return pl.pallas_call(
        functools.partial(_post_attn_kernel, route=route),
        grid=(T // tm,),
        in_specs=in_specs,
        out_specs=out_specs,
        out_shape=out_shape,
        scratch_shapes=scratch,
        compiler_params=_params("arbitrary"),
        name="post_attn_route" if route else "post_attn",
    )(*ins)


def _swiglu_accumulate(h, w1_ref, w3_ref, w2_ref, o_ref):
    a = _dot(h, w1_ref[...])
    g = _dot(h, w3_ref[...])
    act = (a * jax.nn.sigmoid(a) * g).astype(BF16)
    o_ref[...] += _dot(act, w2_ref[...])


def _ffn_kernel(h_ref, res_ref, w1_ref, w3_ref, w2_ref, o_ref):
    @pl.when(pl.program_id(1) == 0)
    def _():
        o_ref[...] = res_ref[...]

    _swiglu_accumulate(h_ref[...], w1_ref, w3_ref, w2_ref, o_ref)


def _ffn(h, res, w1, w3, w2, layer):
    R, D = h.shape
    tm, tf = TM_FFN, TF_FFN
    row = pl.BlockSpec((tm, D), lambda i, f: (i, 0))
    w13 = pl.BlockSpec((None, None, D, tf), lambda i, f: (layer, 0, 0, f))
    w2s = pl.BlockSpec((None, None, tf, D), lambda i, f: (layer, 0, f, 0))
    return pl.pallas_call(
        _ffn_kernel,
        grid=(R // tm, FFN_DIM // tf),
        in_specs=[row, row, w13, w13, w2s],
        out_specs=row,
        out_shape=jax.ShapeDtypeStruct((R, D), F32),
        compiler_params=_params("parallel", "arbitrary"),
        name="swiglu_res",
    )(h, res, w1, w3, w2)


def _moe_ffn_kernel(be_ref, nv_ref, x_ref, w1_ref, w3_ref, w2_ref, o_ref, acc):
    del be_ref
    f = pl.program_id(1)

    @pl.when(f == 0)
    def _():
        acc[...] = jnp.zeros(acc.shape, F32)

    @pl.when(pl.program_id(0) < nv_ref[0])
    def _():
        h = _load_row_tiles(x_ref, TM_FFN).astype(BF16)
        _swiglu_accumulate(h, w1_ref, w3_ref, w2_ref, acc)

    @pl.when(f == pl.num_programs(1) - 1)
    def _():
        _store_row_tiles(o_ref, acc[...])


def _moe_ffn(xs, w1, w3, w2, layer, block_expert, n_valid):
    R = xs.shape[0] // SUBLANES
    D = D_MODEL
    tm, tf = TM_FFN, TF_FFN
    n_f = FFN_DIM // tf
    last = lambda i, nv: jnp.minimum(i, nv[0] - 1)
    fcl = lambda i, f, nv: jnp.where(i < nv[0], f, n_f - 1)
    w13 = pl.BlockSpec((None, None, D, tf),
                       lambda i, f, be, nv: (layer, be[last(i, nv)], 0, fcl(i, f, nv)))
    w2s = pl.BlockSpec((None, None, tf, D),
                       lambda i, f, be, nv: (layer, be[last(i, nv)], fcl(i, f, nv), 0))
    rows = pl.BlockSpec((tm * SUBLANES, LANES), lambda i, f, be, nv: (last(i, nv), 0))
    return pl.pallas_call(
        _moe_ffn_kernel,
        grid_spec=pltpu.PrefetchScalarGridSpec(
            num_scalar_prefetch=2, grid=(R // tm, n_f),
            in_specs=[rows, w13, w13, w2s],
            out_specs=pl.BlockSpec((tm * SUBLANES, LANES), lambda i, f, be, nv: (i, 0)),
            scratch_shapes=[pltpu.VMEM((tm, D), F32)]),
        out_shape=jax.ShapeDtypeStruct((R * SUBLANES, LANES), F32),
        compiler_params=_params("parallel", "arbitrary"),
        name="swiglu_expert",
    )(block_expert, n_valid, xs, w1, w3, w2)


def _start_rows(copy, n):
    def body(r, c):
        for k in range(TOP_K):
            copy(r, k).start(priority=k)
        return c
    lax.fori_loop(0, n, body, 0, unroll=MOVE_UNROLL)


def _wait_rows(copy, n):
    def body(r, c):
        for k in range(TOP_K):
            copy(r, k).wait()
        return c
    lax.fori_loop(0, n, body, 0, unroll=MOVE_UNROLL)


def _dispatch_kernel(pend_ref, dest_ref, h_ref, xs_out, zbuf, sem, zsem):
    blk = TM_FFN * SUBLANES

    @pl.when(pl.program_id(0) == 0)
    def _():
        zbuf[...] = jnp.zeros(zbuf.shape, F32)

        def zero_block(first_row):
            start = pl.multiple_of(first_row * SUBLANES, SUBLANES)
            zero = pltpu.make_async_copy(zbuf, xs_out.at[pl.ds(start, blk)], zsem)
            zero.start()
            zero.wait()

        for e in range(N_EXPERTS):
            @pl.when(pend_ref[e] > 0)
            def _():
                zero_block(pend_ref[e] - TM_FFN)

            unused = pend_ref[N_EXPERTS - 1] + e * TM_FFN

            @pl.when(unused * SUBLANES < xs_out.shape[0])
            def _():
                zero_block(unused)

    def copy(r, k):
        src = h_ref.at[pl.ds(pl.multiple_of(r * SUBLANES, SUBLANES), SUBLANES)]
        dst = xs_out.at[pl.ds(pl.multiple_of(dest_ref[TOP_K * r + k], SUBLANES), SUBLANES)]
        return pltpu.make_async_copy(src, dst, sem)

    _start_rows(copy, TM_DISPATCH)
    _wait_rows(copy, TM_DISPATCH)


def _dispatch(h, dest_flat, pends, n_rows):
    tm = TM_DISPATCH
    T = h.shape[0] // SUBLANES
    return pl.pallas_call(
        _dispatch_kernel,
        grid_spec=pltpu.PrefetchScalarGridSpec(
            num_scalar_prefetch=1, grid=(T // tm,),
            in_specs=[pl.BlockSpec((TOP_K * tm,), lambda i, pe: (i,), memory_space=pltpu.SMEM),
                      pl.BlockSpec((tm * SUBLANES, LANES), lambda i, pe: (i, 0))],
            out_specs=pl.BlockSpec(memory_space=pl.ANY),
            scratch_shapes=[pltpu.VMEM((TM_FFN * SUBLANES, LANES), F32),
                            pltpu.SemaphoreType.DMA(()), pltpu.SemaphoreType.DMA(())]),
        out_shape=jax.ShapeDtypeStruct((n_rows * SUBLANES, LANES), F32),
        compiler_params=_params("arbitrary"),
        name="moe_dispatch",
    )(pends, dest_flat * SUBLANES, h)


def _combine_kernel(dest_ref, next_ref, x_ref, gate_ref, g_ref, ys_hbm, o_ref, buf, sems, *, final):
    tm = TM_MOVE
    i = pl.program_id(0)
    slot = i % 2

    def copies(idx_ref, s):
        def copy(r, k):
            src = ys_hbm.at[pl.ds(pl.multiple_of(idx_ref[TOP_K * r + k], SUBLANES), SUBLANES)]
            dst = buf.at[s, k, pl.ds(pl.multiple_of(r * SUBLANES, SUBLANES), SUBLANES)]
            return pltpu.make_async_copy(src, dst, sems.at[s])
        return copy

    @pl.when(i == 0)
    def _():
        _start_rows(copies(dest_ref, slot), tm)

    @pl.when(i + 1 < pl.num_programs(0))
    def _():
        _start_rows(copies(next_ref, 1 - slot), tm)

    _wait_rows(copies(dest_ref, slot), tm)
    gates = gate_ref[...]
    y = x_ref[...] + (_load_row_tiles(buf.at[slot, 0], tm) * gates[:, 0:1]
                      + _load_row_tiles(buf.at[slot, 1], tm) * gates[:, 1:2])
    o_ref[...] = _rms(y, g_ref[...]) if final else y


def _combine(x, ys, gates, dest_flat, final_g):
    T, D = x.shape
    tm = TM_MOVE
    n = T // tm
    final = final_g is not None
    g = final_g if final else jnp.ones((1, D), F32)
    tile_row = dest_flat * SUBLANES
    return pl.pallas_call(
        functools.partial(_combine_kernel, final=final),
        grid=(n,),
        in_specs=[pl.BlockSpec((TOP_K * tm,), lambda i: (i,), memory_space=pltpu.SMEM),
                  pl.BlockSpec((TOP_K * tm,), lambda i: (jnp.minimum(i + 1, n - 1),),
                               memory_space=pltpu.SMEM),
                  pl.BlockSpec((tm, D), lambda i: (i, 0)),
                  pl.BlockSpec((tm, LANES), lambda i: (i, 0)),
                  pl.BlockSpec((1, D), lambda i: (0, 0)),
                  pl.BlockSpec(memory_space=pl.ANY)],
        out_specs=pl.BlockSpec((tm, D), lambda i: (i, 0)),
        out_shape=jax.ShapeDtypeStruct((T, D), F32),
        scratch_shapes=[pltpu.VMEM((2, TOP_K, tm * SUBLANES, LANES), F32),
                        pltpu.SemaphoreType.DMA((2,))],
        compiler_params=_params("arbitrary"),
        name="moe_combine_final" if final else "moe_combine",
    )(tile_row, tile_row, x, gates, g, ys)


def _final_norm_kernel(x_ref, g_ref, o_ref):
    o_ref[...] = _rms(x_ref[...], g_ref[...])


def _final_norm(x, g):
    T, D = x.shape
    tm = TM_PROJ
    return pl.pallas_call(
        _final_norm_kernel,
        grid=(T // tm,),
        in_specs=[pl.BlockSpec((tm, D), lambda i: (i, 0)), pl.BlockSpec((1, D), lambda i: (0, 0))],
        out_specs=pl.BlockSpec((tm, D), lambda i: (i, 0)),
        out_shape=jax.ShapeDtypeStruct((T, D), F32),
        compiler_params=_params("parallel"),
        name="final_norm",
    )(x, g)


def _rope_tables(seq):
    inv = 1.0 / (ROPE_THETA ** (jnp.arange(0, MLA_ROPE, 2, dtype=F32) / MLA_ROPE))
    ang = jnp.arange(seq, dtype=F32)[:, None] * inv[None, :]
    cos, sin = jnp.cos(ang), jnp.sin(ang)
    ones = jnp.ones((seq, MLA_NOPE), F32)
    z_nope = jnp.zeros((seq, MLA_NOPE), F32)
    z_half = jnp.zeros((seq, HALF_ROPE), F32)
    z_pad = jnp.zeros((seq, LANES - MLA_NOPE - MLA_ROPE), F32)
    scale = (MLA_NOPE + MLA_ROPE) ** -0.5 * np.log2(np.e)
    ck = jnp.concatenate([z_nope, cos, cos, z_pad], axis=1)
    sin_up = jnp.concatenate([z_nope, z_half, sin, z_pad], axis=1)
    sin_dn = jnp.concatenate([z_nope, -sin, z_half, z_pad], axis=1)
    cq = jnp.concatenate([ones, cos, cos, z_pad], axis=1) * scale
    return cq, sin_up * scale, sin_dn * scale, ck, sin_up, sin_dn


def _projection_weights(attn_norm, w_in, q_norm, w_uq, kv_norm, w_ukv, w_out, ffn_norm, w_router):
    depth, d = w_in.shape[:2]
    o = np.cumsum([0, Q_LORA, KV_LORA, MLA_ROPE])
    w_cq, w_ckv, w_kr = (w_in[..., o[j]:o[j + 1]] for j in range(3))
    w_qkvb = w_in[..., o[3]:]
    z_nope = jnp.zeros((depth, d, MLA_NOPE), F32)
    z_pad = jnp.zeros((depth, d, LANES - MLA_NOPE - MLA_ROPE), F32)
    w_kr = jnp.concatenate([z_nope, w_kr, z_pad], axis=-1)

    wq = w_uq.reshape(depth, Q_LORA, MLA_HEADS, MLA_NOPE + MLA_ROPE)
    zq_pad = jnp.zeros((depth, Q_LORA, MLA_HEADS, LANES - MLA_NOPE - MLA_ROPE), F32)
    w_uq = jnp.concatenate([wq, zq_pad], axis=-1).reshape(depth, Q_LORA, MLA_HEADS * LANES)

    wkv = w_ukv.reshape(depth, KV_LORA, MLA_HEADS, MLA_NOPE + MLA_V)
    zk = jnp.zeros((depth, KV_LORA, MLA_HEADS, LANES - MLA_NOPE), F32)
    wide = (depth, KV_LORA, MLA_HEADS * LANES)
    w_uk = jnp.concatenate([wkv[..., :MLA_NOPE], zk], axis=-1).reshape(wide)
    zv = jnp.zeros((depth, KV_LORA, MLA_HEADS, LANES - MLA_V), F32)
    w_uv = jnp.concatenate([wkv[..., MLA_NOPE:], zv], axis=-1).reshape(wide)
    half = MLA_HEADS * MLA_V

    router = jnp.zeros((w_router.shape[0], d, LANES), F32).at[..., :N_EXPERTS].set(w_router)
    router_hi = router.astype(BF16)
    router_lo = (router - router_hi.astype(F32)).astype(BF16)
    return {
        'attn_norm': attn_norm[:, None], 'w_cq': w_cq.astype(BF16), 'w_ckv': w_ckv.astype(BF16),
        'w_kr': w_kr.astype(BF16), 'w_qkvb': w_qkvb.astype(BF16), 'q_norm': q_norm[:, None],
        'w_uq': w_uq.astype(BF16), 'kv_norm': kv_norm[:, None],
        'w_uk': w_uk.astype(BF16), 'w_uv': w_uv.astype(BF16),
        'w_out_a': w_out[:, :half].astype(BF16), 'w_out_b': w_out[:, half:].astype(BF16),
        'ffn_norm': ffn_norm[:, None],
        'w_router': jnp.stack([router_hi, router_lo], axis=1),
    }


def _moe_layout(meta, counts):
    T = meta.shape[0]
    blk = TM_FFN
    cnt = counts[0, :N_EXPERTS].astype(jnp.int32)
    padded = (cnt + blk - 1) // blk * blk
    pends = jnp.cumsum(padded).astype(jnp.int32)
    pstarts = pends - padded
    dest = (pstarts[meta[:, :TOP_K]] + meta[:, TOP_K:2 * TOP_K]).reshape(-1).astype(jnp.int32)
    n_rows = T * TOP_K + N_EXPERTS * blk
    n_blocks = n_rows // blk
    block_expert = jnp.minimum(
        jnp.searchsorted(pends, jnp.arange(n_blocks, dtype=jnp.int32) * blk, side='right'),
        N_EXPERTS - 1).astype(jnp.int32)
    n_valid = pends[-1:] // blk
    return dest, pends, block_expert, n_valid, n_rows


def kernel(x, attn_norm, w_in, q_norm, w_uq, kv_norm, w_ukv, rel_bias, w_out, ffn_norm, dense_w1,
           dense_w3, dense_w2, w_router, moe_w1, moe_w3, moe_w2, final_norm):
    B, S, D = x.shape
    T = B * S
    depth = w_in.shape[0]
    tabs = _rope_tables(S)
    bias = _bias_tiles(_rel_bias_band(rel_bias))
    dense_w = [w.astype(BF16)[:, None] for w in (dense_w1, dense_w3, dense_w2)]
    moe_w = [w.astype(BF16) for w in (moe_w1, moe_w3, moe_w2)]
    lw = _projection_weights(attn_norm, w_in, q_norm, w_uq, kv_norm, w_ukv, w_out, ffn_norm, w_router)
    out = None
    for i in range(depth):
        q, k, v, qb, kb, vb = _pre_attn(x, lw, tabs, i)
        a = _mla_attention(q, k, v)
        b = _chunk_attention(qb, kb, vb, bias, i)
        xt = x.reshape(T, D)
        a2, b2 = a.reshape(T, -1), b.reshape(T, -1)
        j = i // 2
        last = i == depth - 1
        if i % 2 == 0:
            x2, h = _post_attn(xt, a2, b2, lw, i, route=False)
            y = _ffn(h, x2, *dense_w, j)
            if last:
                out = _final_norm(y, final_norm[None])
        else:
            x2, h, meta, gates, counts = _post_attn(xt, a2, b2, lw, i, route=True)
            dest, pends, block_expert, n_valid, n_rows = _moe_layout(meta, counts)
            xs = _dispatch(h, dest, pends, n_rows)
            ys = _moe_ffn(xs, *moe_w, j, block_expert, n_valid)
            y = _combine(x2, ys, gates, dest, final_norm[None] if last else None)
            if last:
                out = y
        x = y.reshape(B, S, D)
    return out.reshape(B, S, D)
```

```python
import functools

import jax
import jax.numpy as jnp
import numpy as np
from jax import lax
from jax.experimental import pallas as pl
from jax.experimental.pallas import tpu as pltpu

D_MODEL = 1024
CHUNK = 64
MLA_HEADS = 8
MLA_NOPE = 64
MLA_ROPE = 32
MLA_V = 64
Q_LORA = 384
KV_LORA = 256
ROPE_THETA = 10000.0
CA_HEADS = 8
CA_HEAD_DIM = 64
LEFT_CHUNKS = 8
REL_CLIP = 128
FFN_DIM = 3584
N_EXPERTS = 8
TOP_K = 2
EPS = 1e-6
NEG_INF = -1e30

CA_DIM = CA_HEADS * CA_HEAD_DIM
BAND = LEFT_CHUNKS + 1
HALF_ROPE = MLA_ROPE // 2
LANES = 128
SUBLANES = 8
assert D_MODEL == SUBLANES * LANES
VMEM_LIMIT = 56 * 1024 * 1024

TM_PROJ = 512
ROUTE_CHUNK = 128
TQ_MLA = 1024
TK_MLA = 512
TQ_CA = 256
CA_WIN = LEFT_CHUNKS * CHUNK + TQ_CA
N_CA_LEAD = LEFT_CHUNKS * CHUNK // TQ_CA
TM_FFN = 512
TF_FFN = 1792
TM_DISPATCH = 1024
TM_MOVE = 512
MOVE_UNROLL = 8
TABLE_UNROLL = 8

BF16 = jnp.bfloat16
F32 = jnp.float32


def _params(*sem):
    return pltpu.CompilerParams(dimension_semantics=sem, vmem_limit_bytes=VMEM_LIMIT)


def _rms(x, g):
    ms = jnp.mean(x * x, axis=-1, keepdims=True)
    return x * lax.rsqrt(ms + EPS) * g


def _dot(a, b):
    return jnp.dot(a, b, preferred_element_type=F32)


def _store_row_tiles(ref, value):
    n = value.shape[0]
    for s in range(SUBLANES):
        ref[pl.ds(s, n, stride=SUBLANES), :] = value[:, s * LANES:(s + 1) * LANES]


def _load_row_tiles(ref, n):
    return jnp.concatenate([ref[pl.ds(s, n, stride=SUBLANES), :] for s in range(SUBLANES)], axis=1)


def _dot_nt(a, b):
    return lax.dot_general(a, b, (((1,), (1,)), ((), ())), preferred_element_type=F32)


def _rotary(x, cos_t, sin_up, sin_dn):
    up = pltpu.roll(x, HALF_ROPE, axis=1)
    down = pltpu.roll(x, LANES - HALF_ROPE, axis=1)
    return x * cos_t + up * sin_up + down * sin_dn


def _pre_attn_kernel(x_ref, an_ref, wcq_ref, wckv_ref, wkr_ref, wqkvb_ref, qn_ref, wuq_ref,
                     kvn_ref, wuk_ref, wuv_ref, cq_ref, squ_ref, sqd_ref, ck_ref, sku_ref, skd_ref,
                     q_out, k_out, v_out, qb_out, kb_out, vb_out):
    h = _rms(x_ref[...], an_ref[...]).astype(BF16)
    c_q = _dot(h, wcq_ref[...])
    c_kv = _dot(h, wckv_ref[...])
    k_r = _dot(h, wkr_ref[...])
    qkvb = _dot(h, wqkvb_ref[...])
    qb_out[...] = (qkvb[:, :CA_DIM] * (CA_HEAD_DIM ** -0.5 * np.log2(np.e))).astype(BF16)
    kb_out[...] = qkvb[:, CA_DIM:2 * CA_DIM].astype(BF16)
    vb_out[...] = qkvb[:, 2 * CA_DIM:].astype(BF16)

    qp = _dot(_rms(c_q, qn_ref[...]).astype(BF16), wuq_ref[...])
    cq, squ, sqd = cq_ref[...], squ_ref[...], sqd_ref[...]
    for hd in range(MLA_HEADS):
        q_out[hd] = _rotary(qp[:, hd * LANES:(hd + 1) * LANES], cq, squ, sqd).astype(BF16)

    ckvn = _rms(c_kv, kvn_ref[...]).astype(BF16)
    kn = _dot(ckvn, wuk_ref[...])
    vv = _dot(ckvn, wuv_ref[...])
    kr = _rotary(k_r, ck_ref[...], sku_ref[...], skd_ref[...])
    lane = lax.broadcasted_iota(jnp.int32, kr.shape, 1)
    one = jnp.where(lane == MLA_V, 1.0, 0.0)
    for hd in range(MLA_HEADS):
        k_out[hd] = (kn[:, hd * LANES:(hd + 1) * LANES] + kr).astype(BF16)
        v_out[hd] = (vv[:, hd * LANES:(hd + 1) * LANES] + one).astype(BF16)


def _pre_attn(x, lw, tabs, layer):
    B, S, D = x.shape
    tm = TM_PROJ
    n_s = S // tm
    full = lambda a: pl.BlockSpec((None,) + a.shape[1:], lambda b, i: (layer,) + (0,) * (a.ndim - 1))
    tab = pl.BlockSpec((tm, LANES), lambda b, i: (i, 0))
    tok = lambda w: pl.BlockSpec((None, tm, w), lambda b, i: (b, i, 0))
    heads = lambda n: pl.BlockSpec((None, n, tm, LANES), lambda b, i: (b, 0, i, 0))
    weights = [lw['attn_norm'], lw['w_cq'], lw['w_ckv'], lw['w_kr'], lw['w_qkvb'], lw['q_norm'],
               lw['w_uq'], lw['kv_norm'], lw['w_uk'], lw['w_uv']]
    return pl.pallas_call(
        _pre_attn_kernel,
        grid=(B, n_s),
        in_specs=[tok(D)] + [full(w) for w in weights] + [tab] * len(tabs),
        out_specs=[heads(MLA_HEADS), heads(MLA_HEADS), heads(MLA_HEADS),
                   tok(CA_DIM), tok(CA_DIM), tok(CA_DIM)],
        out_shape=[jax.ShapeDtypeStruct((B, MLA_HEADS, S, LANES), BF16),
                   jax.ShapeDtypeStruct((B, MLA_HEADS, S, LANES), BF16),
                   jax.ShapeDtypeStruct((B, MLA_HEADS, S, LANES), BF16),
                   jax.ShapeDtypeStruct((B, S, CA_DIM), BF16),
                   jax.ShapeDtypeStruct((B, S, CA_DIM), BF16),
                   jax.ShapeDtypeStruct((B, S, CA_DIM), BF16)],
        compiler_params=_params("parallel", "parallel"),
        name="pre_attn",
    )(x, *weights, *tabs)


def _mla_kernel(qi_ref, kj_ref, q_ref, k_ref, v_ref, o_ref, m_scr, acc_scr, *, tq, tk):
    t = pl.program_id(1)
    kj = kj_ref[t]
    ratio = tq // tk
    diag = kj - qi_ref[t] * ratio

    @pl.when(kj == 0)
    def _():
        m_scr[...] = jnp.full(m_scr.shape, NEG_INF, F32)
        acc_scr[...] = jnp.zeros(acc_scr.shape, F32)

    def update(hd, lo, n, visible):
        s = _dot_nt(q_ref[hd, lo:lo + n], k_ref[hd])
        if visible is not None:
            s = jnp.where(visible, s, NEG_INF)
        m_prev = m_scr[hd, lo:lo + n]
        m_new = jnp.maximum(m_prev, jnp.max(s, axis=1, keepdims=True))
        p = jnp.exp2((s - jnp.tile(m_new, (1, tk // LANES))).astype(BF16))
        m_scr[hd, lo:lo + n] = m_new
        acc_scr[hd, lo:lo + n] = (acc_scr[hd, lo:lo + n] * jnp.exp2(m_prev - m_new)
                                  + _dot(p, v_ref[hd]))

    @pl.when(diag < 0)
    def _():
        for hd in range(MLA_HEADS):
            update(hd, 0, tq, None)

    for d in range(ratio):
        @pl.when(diag == d)
        def _():
            n = tq - d * tk
            row = lax.broadcasted_iota(jnp.int32, (n, tk), 0) // CHUNK
            col = lax.broadcasted_iota(jnp.int32, (n, tk), 1) // CHUNK
            visible = col <= row
            for hd in range(MLA_HEADS):
                update(hd, d * tk, n, visible)

    @pl.when(diag == ratio - 1)
    def _():
        lane = lax.broadcasted_iota(jnp.int32, (tq, LANES), 1)

        def normalised(acc):
            denom = jnp.sum(jnp.where(lane == MLA_V, acc, 0.0), axis=1, keepdims=True)
            return acc * (1.0 / denom)

        for p in range(MLA_HEADS // 2):
            even, odd = normalised(acc_scr[2 * p]), normalised(acc_scr[2 * p + 1])
            pair = jnp.where(lane < MLA_V, even, pltpu.roll(odd, MLA_V, axis=1))
            o_ref[:, p * LANES:(p + 1) * LANES] = pair.astype(BF16)


def _mla_attention(q, k, v):
    B, H, S, _ = q.shape
    tq, tk = TQ_MLA, TK_MLA
    ratio = tq // tk
    pairs = [(i, j) for i in range(S // tq) for j in range(ratio * (i + 1))]
    qi = jnp.asarray(np.array([p[0] for p in pairs], np.int32))
    kj = jnp.asarray(np.array([p[1] for p in pairs], np.int32))
    kv_spec = pl.BlockSpec((None, H, tk, LANES), lambda b, t, qi, kj: (b, 0, kj[t], 0))
    grid_spec = pltpu.PrefetchScalarGridSpec(
        num_scalar_prefetch=2,
        grid=(B, len(pairs)),
        in_specs=[pl.BlockSpec((None, H, tq, LANES), lambda b, t, qi, kj: (b, 0, qi[t], 0)),
                  kv_spec, kv_spec],
        out_specs=pl.BlockSpec((None, tq, H * MLA_V), lambda b, t, qi, kj: (b, qi[t], 0)),
        scratch_shapes=[pltpu.VMEM((H, tq, LANES), F32)] * 2,
    )
    return pl.pallas_call(
        functools.partial(_mla_kernel, tq=tq, tk=tk),
        grid_spec=grid_spec,
        out_shape=jax.ShapeDtypeStruct((B, S, H * MLA_V), BF16),
        compiler_params=_params("parallel", "arbitrary"),
        name="mla_attention",
    )(qi, kj, q, k, v)


def _rel_bias_kernel(tab_ref, o_ref):
    far = max(0, (LEFT_CHUNKS * CHUNK - REL_CLIP) // LANES * LANES)
    near = BAND * CHUNK - far
    r = lax.broadcasted_iota(jnp.int32, (CHUNK, near), 0)
    c = lax.broadcasted_iota(jnp.int32, (CHUNK, near), 1) + far
    rel = jnp.clip(LEFT_CHUNKS * CHUNK + r - c, -REL_CLIP, REL_CLIP) + REL_CLIP
    t_lo = REL_CLIP + max(-REL_CLIP, LEFT_CHUNKS * CHUNK - (BAND * CHUNK - 1))
    t_hi = REL_CLIP + min(REL_CLIP, LEFT_CHUNKS * CHUNK + CHUNK - 1)
    for hd in range(CA_HEADS):
        def body(t, acc):
            return jnp.where(rel == t, tab_ref[hd, t], acc)
        o_ref[hd, :, far:] = lax.fori_loop(t_lo, t_hi + 1, body, jnp.zeros(rel.shape, F32),
                                           unroll=TABLE_UNROLL)
        if far:
            o_ref[hd, :, :far] = jnp.full((CHUNK, far), tab_ref[hd, 2 * REL_CLIP], F32)


def _rel_bias_band(rel_bias):
    depth = rel_bias.shape[0]
    return pl.pallas_call(
        _rel_bias_kernel,
        grid=(depth,),
        in_specs=[pl.BlockSpec((None, CA_HEADS, 2 * REL_CLIP + 1), lambda d: (d, 0, 0),
                               memory_space=pltpu.SMEM)],
        out_specs=pl.BlockSpec((None, CA_HEADS, CHUNK, BAND * CHUNK), lambda d: (d, 0, 0, 0)),
        out_shape=jax.ShapeDtypeStruct((depth, CA_HEADS, CHUNK, BAND * CHUNK), F32),
        compiler_params=_params("parallel"),
        name="rel_bias_band",
    )(rel_bias)


def _bias_tiles(band):
    depth = band.shape[0]
    n_qc = TQ_CA // CHUNK
    n_kc = CA_WIN // CHUNK
    neg = jnp.full((depth, CA_HEADS, CHUNK, CHUNK), NEG_INF, F32)
    rows = []
    for rc in range(n_qc):
        blocks = []
        for cc in range(n_kc):
            j = cc - rc
            blocks.append(band[..., j * CHUNK:(j + 1) * CHUNK] if 0 <= j < BAND else neg)
        rows.append(jnp.concatenate(blocks, axis=-1))
    tile = jnp.concatenate(rows, axis=-2) * np.log2(np.e)
    col = jnp.arange(CA_WIN)
    variants = [jnp.where(col >= LEFT_CHUNKS * CHUNK - i * TQ_CA, tile, NEG_INF)
                for i in range(N_CA_LEAD)] + [tile]
    return jnp.stack(variants, axis=1)


def _chunk_attn_kernel(q_ref, k2_ref, k1_ref, k0_ref, v2_ref, v1_ref, v0_ref, bias_ref, o_ref):
    tq, win = TQ_CA, CA_WIN
    lane_q = lax.broadcasted_iota(jnp.int32, (tq, LANES), 1)
    first = lane_q < CA_HEAD_DIM
    for p in range(CA_HEADS // 2):
        sl = slice(p * LANES, (p + 1) * LANES)
        q = q_ref[:, sl]
        k = jnp.concatenate([k2_ref[:, sl], k1_ref[:, sl], k0_ref[:, sl]], axis=0)
        v = jnp.concatenate([v2_ref[:, sl], v1_ref[:, sl], v0_ref[:, sl]], axis=0)
        zero = jnp.zeros_like(q)
        q2 = jnp.concatenate([jnp.where(first, q, zero), jnp.where(first, zero, q)], axis=0)
        s = _dot_nt(q2, k) + bias_ref[2 * p:2 * p + 2].reshape(2 * tq, win)
        m = jnp.max(s, axis=1, keepdims=True)
        e = jnp.exp2(s - m)
        l = jnp.sum(e, axis=1, keepdims=True)
        o = _dot(e.astype(BF16), v) / l
        o_ref[:, sl] = jnp.where(first, o[:tq], o[tq:]).astype(BF16)


def _chunk_attention(qb, kb, vb, bias, layer):
    B, S, _ = qb.shape
    tq = TQ_CA
    n = S // tq
    cur = pl.BlockSpec((None, tq, CA_DIM), lambda b, i: (b, i, 0))
    prev1 = pl.BlockSpec((None, tq, CA_DIM), lambda b, i: (b, jnp.maximum(i - 1, 0), 0))
    prev2 = pl.BlockSpec((None, tq, CA_DIM), lambda b, i: (b, jnp.maximum(i - 2, 0), 0))
    return pl.pallas_call(
        _chunk_attn_kernel,
        grid=(B, n),
        in_specs=[cur, prev2, prev1, cur, prev2, prev1, cur,
                  pl.BlockSpec((None, None) + bias.shape[2:],
                               lambda b, i: (layer, jnp.minimum(i, N_CA_LEAD), 0, 0, 0))],
        out_specs=cur,
        out_shape=jax.ShapeDtypeStruct((B, S, CA_DIM), BF16),
        compiler_params=_params("parallel", "parallel"),
        name="chunk_attention",
    )(qb, kb, kb, kb, vb, vb, vb, bias)


def _post_attn_kernel(x_ref, a_ref, b_ref, woa_ref, wob_ref, fn_ref, *rest, route):
    if route:
        wr_ref, x2_out, h_out, meta_out, gate_out, cnt_out, run_scr = rest
    else:
        x2_out, h_out = rest
    ab = jnp.concatenate([a_ref[...], b_ref[...]], axis=1)
    wo = jnp.concatenate([woa_ref[...], wob_ref[...]], axis=0)
    x2 = x_ref[...] + _dot(ab, wo)
    x2_out[...] = x2
    h = _rms(x2, fn_ref[...])
    if not route:
        h_out[...] = h.astype(h_out.dtype)
        return
    _store_row_tiles(h_out, h)

    tm = x2.shape[0]
    step = pl.program_id(0)

    @pl.when(step == 0)
    def _():
        run_scr[...] = jnp.zeros(run_scr.shape, F32)

    h_hi = h.astype(BF16)
    h_lo = (h - h_hi.astype(F32)).astype(BF16)
    all_logits = _dot(jnp.concatenate([h_hi, h_lo, h_hi], axis=1),
                      jnp.concatenate([wr_ref[0], wr_ref[0], wr_ref[1]], axis=0))

    rc = ROUTE_CHUNK
    lane = lax.broadcasted_iota(jnp.int32, (rc, LANES), 1)
    r_i = lax.broadcasted_iota(jnp.int32, (rc, rc), 0)
    c_i = lax.broadcasted_iota(jnp.int32, (rc, rc), 1)
    lower = jnp.where(c_i < r_i, 1.0, 0.0).astype(BF16)
    run = run_scr[...]
    for c in range(tm // rc):
        rows = slice(c * rc, (c + 1) * rc)
        logits = jnp.where(lane < N_EXPERTS, all_logits[rows], -jnp.inf)
        v1 = jnp.max(logits, axis=1, keepdims=True)
        e1 = jnp.min(jnp.where(logits == v1, lane, LANES), axis=1, keepdims=True)
        rest_logits = jnp.where(lane == e1, -jnp.inf, logits)
        v2 = jnp.max(rest_logits, axis=1, keepdims=True)
        e2 = jnp.min(jnp.where(rest_logits == v2, lane, LANES), axis=1, keepdims=True)
        w2 = jnp.exp(v2 - v1)
        g1 = 1.0 / (1.0 + w2)
        g2 = w2 / (1.0 + w2)

        sel1 = lane == e1
        sel2 = lane == e2
        cnt = jnp.where(sel1 | sel2, 1.0, 0.0)
        before = _dot(lower, cnt.astype(BF16)) + run
        rank1 = jnp.sum(jnp.where(sel1, before, 0.0), axis=1, keepdims=True)
        rank2 = jnp.sum(jnp.where(sel2, before, 0.0), axis=1, keepdims=True)
        run = run + jnp.sum(cnt, axis=0, keepdims=True)

        meta_out[rows] = jnp.where(lane == 0, e1, jnp.where(lane == 1, e2, jnp.where(
            lane == 2, rank1.astype(jnp.int32), jnp.where(lane == 3, rank2.astype(jnp.int32), 0))))
        gate_out[rows] = jnp.where(lane == 0, g1, jnp.where(lane == 1, g2, 0.0))
    run_scr[...] = run
    cnt_out[...] = run


def _post_attn(x, a, b, lw, layer, route):
    T, D = x.shape
    tm = TM_PROJ
    half = a.shape[1]
    tok = lambda w: pl.BlockSpec((tm, w), lambda i: (i, 0))
    stacked = lambda arr, l: pl.BlockSpec((None,) + arr.shape[1:], lambda i: (l,) + (0,) * (arr.ndim - 1))
    full = lambda arr: stacked(arr, layer)
    ins = [x, a, b, lw['w_out_a'], lw['w_out_b'], lw['ffn_norm']]
    in_specs = [tok(D), tok(half), tok(half), full(ins[3]), full(ins[4]), full(ins[5])]
    if route:
        h_spec = pl.BlockSpec((tm * SUBLANES, LANES), lambda i: (i, 0))
        h_shape = jax.ShapeDtypeStruct((T * SUBLANES, LANES), F32)
    else:
        h_spec, h_shape = tok(D), jax.ShapeDtypeStruct((T, D), BF16)
    out_specs = [tok(D), h_spec]
    out_shape = [jax.ShapeDtypeStruct((T, D), F32), h_shape]
    scratch = []
    if route:
        ins.append(lw['w_router'])
        in_specs.append(stacked(lw['w_router'], layer // 2))
        out_specs += [tok(LANES), tok(LANES), pl.BlockSpec((1, LANES), lambda i: (0, 0))]
        out_shape += [jax.ShapeDtypeStruct((T, LANES), jnp.int32),
                      jax.ShapeDtypeStruct((T, LANES), F32),
                      jax.ShapeDtypeStruct((1, LANES), F32)]
        scratch = [pltpu.VMEM((1, LANES), F32)]
    return pl.pallas_call(
        functools.partial(_post_attn_kernel, route=route),
        grid=(T // tm,),
        in_specs=in_specs,
        out_specs=out_specs,
        out_shape=out_shape,
        scratch_shapes=scratch,
        compiler_params=_params("arbitrary"),
        name="post_attn_route" if route else "post_attn",
    )(*ins)


def _swiglu_accumulate(h, w1_ref, w3_ref, w2_ref, o_ref):
    a = _dot(h, w1_ref[...])
    g = _dot(h, w3_ref[...])
    act = (a * jax.nn.sigmoid(a) * g).astype(BF16)
    o_ref[...] += _dot(act, w2_ref[...])


def _ffn_kernel(h_ref, res_ref, w1_ref, w3_ref, w2_ref, o_ref):
    @pl.when(pl.program_id(1) == 0)
    def _():
        o_ref[...] = res_ref[...]

    _swiglu_accumulate(h_ref[...], w1_ref, w3_ref, w2_ref, o_ref)


def _ffn(h, res, w1, w3, w2, layer):
    R, D = h.shape
    tm, tf = TM_FFN, TF_FFN
    row = pl.BlockSpec((tm, D), lambda i, f: (i, 0))
    w13 = pl.BlockSpec((None, None, D, tf), lambda i, f: (layer, 0, 0, f))
    w2s = pl.BlockSpec((None, None, tf, D), lambda i, f: (layer, 0, f, 0))
    return pl.pallas_call(
        _ffn_kernel,
        grid=(R // tm, FFN_DIM // tf),
        in_specs=[row, row, w13, w13, w2s],
        out_specs=row,
        out_shape=jax.ShapeDtypeStruct((R, D), F32),
        compiler_params=_params("parallel", "arbitrary"),
        name="swiglu_res",
    )(h, res, w1, w3, w2)


def _moe_ffn_kernel(be_ref, nv_ref, x_ref, w1_ref, w3_ref, w2_ref, o_ref, acc):
    del be_ref
    f = pl.program_id(1)

    @pl.when(f == 0)
    def _():
        acc[...] = jnp.zeros(acc.shape, F32)

    @pl.when(pl.program_id(0) < nv_ref[0])
    def _():
        h = _load_row_tiles(x_ref, TM_FFN).astype(BF16)
        _swiglu_accumulate(h, w1_ref, w3_ref, w2_ref, acc)

    @pl.when(f == pl.num_programs(1) - 1)
    def _():
        _store_row_tiles(o_ref, acc[...])


def _moe_ffn(xs, w1, w3, w2, layer, block_expert, n_valid):
    R = xs.shape[0] // SUBLANES
    D = D_MODEL
    tm, tf = TM_FFN, TF_FFN
    n_f = FFN_DIM // tf
    last = lambda i, nv: jnp.minimum(i, nv[0] - 1)
    fcl = lambda i, f, nv: jnp.where(i < nv[0], f, n_f - 1)
    w13 = pl.BlockSpec((None, None, D, tf),
                       lambda i, f, be, nv: (layer, be[last(i, nv)], 0, fcl(i, f, nv)))
    w2s = pl.BlockSpec((None, None, tf, D),
                       lambda i, f, be, nv: (layer, be[last(i, nv)], fcl(i, f, nv), 0))
    rows = pl.BlockSpec((tm * SUBLANES, LANES), lambda i, f, be, nv: (last(i, nv), 0))
    return pl.pallas_call(
        _moe_ffn_kernel,
        grid_spec=pltpu.PrefetchScalarGridSpec(
            num_scalar_prefetch=2, grid=(R // tm, n_f),
            in_specs=[rows, w13, w13, w2s],
            out_specs=pl.BlockSpec((tm * SUBLANES, LANES), lambda i, f, be, nv: (i, 0)),
            scratch_shapes=[pltpu.VMEM((tm, D), F32)]),
        out_shape=jax.ShapeDtypeStruct((R * SUBLANES, LANES), F32),
        compiler_params=_params("parallel", "arbitrary"),
        name="swiglu_expert",
    )(block_expert, n_valid, xs, w1, w3, w2)


def _start_rows(copy, n):
    def body(r, c):
        for k in range(TOP_K):
            copy(r, k).start(priority=k)
        return c
    lax.fori_loop(0, n, body, 0, unroll=MOVE_UNROLL)


def _wait_rows(copy, n):
    def body(r, c):
        for k in range(TOP_K):
            copy(r, k).wait()
        return c
    lax.fori_loop(0, n, body, 0, unroll=MOVE_UNROLL)


def _dispatch_kernel(pend_ref, dest_ref, h_ref, xs_out, zbuf, sem, zsem):
    blk = TM_FFN * SUBLANES

    @pl.when(pl.program_id(0) == 0)
    def _():
        zbuf[...] = jnp.zeros(zbuf.shape, F32)

        def zero_block(first_row):
            start = pl.multiple_of(first_row * SUBLANES, SUBLANES)
            zero = pltpu.make_async_copy(zbuf, xs_out.at[pl.ds(start, blk)], zsem)
            zero.start()
            zero.wait()

        for e in range(N_EXPERTS):
            @pl.when(pend_ref[e] > 0)
            def _():
                zero_block(pend_ref[e] - TM_FFN)

            unused = pend_ref[N_EXPERTS - 1] + e * TM_FFN

            @pl.when(unused * SUBLANES < xs_out.shape[0])
            def _():
                zero_block(unused)

    def copy(r, k):
        src = h_ref.at[pl.ds(pl.multiple_of(r * SUBLANES, SUBLANES), SUBLANES)]
        dst = xs_out.at[pl.ds(pl.multiple_of(dest_ref[TOP_K * r + k], SUBLANES), SUBLANES)]
        return pltpu.make_async_copy(src, dst, sem)

    _start_rows(copy, TM_DISPATCH)
    _wait_rows(copy, TM_DISPATCH)


def _dispatch(h, dest_flat, pends, n_rows):
    tm = TM_DISPATCH
    T = h.shape[0] // SUBLANES
    return pl.pallas_call(
        _dispatch_kernel,
        grid_spec=pltpu.PrefetchScalarGridSpec(
            num_scalar_prefetch=1, grid=(T // tm,),
            in_specs=[pl.BlockSpec((TOP_K * tm,), lambda i, pe: (i,), memory_space=pltpu.SMEM),
                      pl.BlockSpec((tm * SUBLANES, LANES), lambda i, pe: (i, 0))],
            out_specs=pl.BlockSpec(memory_space=pl.ANY),
            scratch_shapes=[pltpu.VMEM((TM_FFN * SUBLANES, LANES), F32),
                            pltpu.SemaphoreType.DMA(()), pltpu.SemaphoreType.DMA(())]),
        out_shape=jax.ShapeDtypeStruct((n_rows * SUBLANES, LANES), F32),
        compiler_params=_params("arbitrary"),
        name="moe_dispatch",
    )(pends, dest_flat * SUBLANES, h)


def _combine_kernel(dest_ref, next_ref, x_ref, gate_ref, g_ref, ys_hbm, o_ref, buf, sems, *, final):
    tm = TM_MOVE
    i = pl.program_id(0)
    slot = i % 2

    def copies(idx_ref, s):
        def copy(r, k):
            src = ys_hbm.at[pl.ds(pl.multiple_of(idx_ref[TOP_K * r + k], SUBLANES), SUBLANES)]
            dst = buf.at[s, k, pl.ds(pl.multiple_of(r * SUBLANES, SUBLANES), SUBLANES)]
            return pltpu.make_async_copy(src, dst, sems.at[s])
        return copy

    @pl.when(i == 0)
    def _():
        _start_rows(copies(dest_ref, slot), tm)

    @pl.when(i + 1 < pl.num_programs(0))
    def _():
        _start_rows(copies(next_ref, 1 - slot), tm)

    _wait_rows(copies(dest_ref, slot), tm)
    gates = gate_ref[...]
    y = x_ref[...] + (_load_row_tiles(buf.at[slot, 0], tm) * gates[:, 0:1]
                      + _load_row_tiles(buf.at[slot, 1], tm) * gates[:, 1:2])
    o_ref[...] = _rms(y, g_ref[...]) if final else y


def _combine(x, ys, gates, dest_flat, final_g):
    T, D = x.shape
    tm = TM_MOVE
    n = T // tm
    final = final_g is not None
    g = final_g if final else jnp.ones((1, D), F32)
    tile_row = dest_flat * SUBLANES
    return pl.pallas_call(
        functools.partial(_combine_kernel, final=final),
        grid=(n,),
        in_specs=[pl.BlockSpec((TOP_K * tm,), lambda i: (i,), memory_space=pltpu.SMEM),
                  pl.BlockSpec((TOP_K * tm,), lambda i: (jnp.minimum(i + 1, n - 1),),
                               memory_space=pltpu.SMEM),
                  pl.BlockSpec((tm, D), lambda i: (i, 0)),
                  pl.BlockSpec((tm, LANES), lambda i: (i, 0)),
                  pl.BlockSpec((1, D), lambda i: (0, 0)),
                  pl.BlockSpec(memory_space=pl.ANY)],
        out_specs=pl.BlockSpec((tm, D), lambda i: (i, 0)),
        out_shape=jax.ShapeDtypeStruct((T, D), F32),
        scratch_shapes=[pltpu.VMEM((2, TOP_K, tm * SUBLANES, LANES), F32),
                        pltpu.SemaphoreType.DMA((2,))],
        compiler_params=_params("arbitrary"),
        name="moe_combine_final" if final else "moe_combine",
    )(tile_row, tile_row, x, gates, g, ys)


def _final_norm_kernel(x_ref, g_ref, o_ref):
    o_ref[...] = _rms(x_ref[...], g_ref[...])


def _final_norm(x, g):
    T, D = x.shape
    tm = TM_PROJ
    return pl.pallas_call(
        _final_norm_kernel,
        grid=(T // tm,),
        in_specs=[pl.BlockSpec((tm, D), lambda i: (i, 0)), pl.BlockSpec((1, D), lambda i: (0, 0))],
        out_specs=pl.BlockSpec((tm, D), lambda i: (i, 0)),
        out_shape=jax.ShapeDtypeStruct((T, D), F32),
        compiler_params=_params("parallel"),
        name="final_norm",
    )(x, g)


def _rope_tables(seq):
    inv = 1.0 / (ROPE_THETA ** (jnp.arange(0, MLA_ROPE, 2, dtype=F32) / MLA_ROPE))
    ang = jnp.arange(seq, dtype=F32)[:, None] * inv[None, :]
    cos, sin = jnp.cos(ang), jnp.sin(ang)
    ones = jnp.ones((seq, MLA_NOPE), F32)
    z_nope = jnp.zeros((seq, MLA_NOPE), F32)
    z_half = jnp.zeros((seq, HALF_ROPE), F32)
    z_pad = jnp.zeros((seq, LANES - MLA_NOPE - MLA_ROPE), F32)
    scale = (MLA_NOPE + MLA_ROPE) ** -0.5 * np.log2(np.e)
    ck = jnp.concatenate([z_nope, cos, cos, z_pad], axis=1)
    sin_up = jnp.concatenate([z_nope, z_half, sin, z_pad], axis=1)
    sin_dn = jnp.concatenate([z_nope, -sin, z_half, z_pad], axis=1)
    cq = jnp.concatenate([ones, cos, cos, z_pad], axis=1) * scale
    return cq, sin_up * scale, sin_dn * scale, ck, sin_up, sin_dn


def _projection_weights(attn_norm, w_in, q_norm, w_uq, kv_norm, w_ukv, w_out, ffn_norm, w_router):
    depth, d = w_in.shape[:2]
    o = np.cumsum([0, Q_LORA, KV_LORA, MLA_ROPE])
    w_cq, w_ckv, w_kr = (w_in[..., o[j]:o[j + 1]] for j in range(3))
    w_qkvb = w_in[..., o[3]:]
    z_nope = jnp.zeros((depth, d, MLA_NOPE), F32)
    z_pad = jnp.zeros((depth, d, LANES - MLA_NOPE - MLA_ROPE), F32)
    w_kr = jnp.concatenate([z_nope, w_kr, z_pad], axis=-1)

    wq = w_uq.reshape(depth, Q_LORA, MLA_HEADS, MLA_NOPE + MLA_ROPE)
    zq_pad = jnp.zeros((depth, Q_LORA, MLA_HEADS, LANES - MLA_NOPE - MLA_ROPE), F32)
    w_uq = jnp.concatenate([wq, zq_pad], axis=-1).reshape(depth, Q_LORA, MLA_HEADS * LANES)

    wkv = w_ukv.reshape(depth, KV_LORA, MLA_HEADS, MLA_NOPE + MLA_V)
    zk = jnp.zeros((depth, KV_LORA, MLA_HEADS, LANES - MLA_NOPE), F32)
    wide = (depth, KV_LORA, MLA_HEADS * LANES)
    w_uk = jnp.concatenate([wkv[..., :MLA_NOPE], zk], axis=-1).reshape(wide)
    zv = jnp.zeros((depth, KV_LORA, MLA_HEADS, LANES - MLA_V), F32)
    w_uv = jnp.concatenate([wkv[..., MLA_NOPE:], zv], axis=-1).reshape(wide)
    half = MLA_HEADS * MLA_V

    router = jnp.zeros((w_router.shape[0], d, LANES), F32).at[..., :N_EXPERTS].set(w_router)
    router_hi = router.astype(BF16)
    router_lo = (router - router_hi.astype(F32)).astype(BF16)
    return {
        'attn_norm': attn_norm[:, None], 'w_cq': w_cq.astype(BF16), 'w_ckv': w_ckv.astype(BF16),
        'w_kr': w_kr.astype(BF16), 'w_qkvb': w_qkvb.astype(BF16), 'q_norm': q_norm[:, None],
        'w_uq': w_uq.astype(BF16), 'kv_norm': kv_norm[:, None],
        'w_uk': w_uk.astype(BF16), 'w_uv': w_uv.astype(BF16),
        'w_out_a': w_out[:, :half].astype(BF16), 'w_out_b': w_out[:, half:].astype(BF16),
        'ffn_norm': ffn_norm[:, None],
        'w_router': jnp.stack([router_hi, router_lo], axis=1),
    }


def _moe_layout(meta, counts):
    T = meta.shape[0]
    blk = TM_FFN
    cnt = counts[0, :N_EXPERTS].astype(jnp.int32)
    padded = (cnt + blk - 1) // blk * blk
    pends = jnp.cumsum(padded).astype(jnp.int32)
    pstarts = pends - padded
    dest = (pstarts[meta[:, :TOP_K]] + meta[:, TOP_K:2 * TOP_K]).reshape(-1).astype(jnp.int32)
    n_rows = T * TOP_K + N_EXPERTS * blk
    n_blocks = n_rows // blk
    block_expert = jnp.minimum(
        jnp.searchsorted(pends, jnp.arange(n_blocks, dtype=jnp.int32) * blk, side='right'),
        N_EXPERTS - 1).astype(jnp.int32)
    n_valid = pends[-1:] // blk
    return dest, pends, block_expert, n_valid, n_rows


def kernel(x, attn_norm, w_in, q_norm, w_uq, kv_norm, w_ukv, rel_bias, w_out, ffn_norm, dense_w1,
           dense_w3, dense_w2, w_router, moe_w1, moe_w3, moe_w2, final_norm):
    B, S, D = x.shape
    T = B * S
    depth = w_in.shape[0]
    tabs = _rope_tables(S)
    bias = _bias_tiles(_rel_bias_band(rel_bias))
    dense_w = [w.astype(BF16)[:, None] for w in (dense_w1, dense_w3, dense_w2)]
    moe_w = [w.astype(BF16) for w in (moe_w1, moe_w3, moe_w2)]
    lw = _projection_weights(attn_norm, w_in, q_norm, w_uq, kv_norm, w_ukv, w_out, ffn_norm, w_router)
    out = None
    for i in range(depth):
        q, k, v, qb, kb, vb = _pre_attn(x, lw, tabs, i)
        a = _mla_attention(q, k, v)
        b = _chunk_attention(qb, kb, vb, bias, i)
        xt = x.reshape(T, D)
        a2, b2 = a.reshape(T, -1), b.reshape(T, -1)
        j = i // 2
        last = i == depth - 1
        if i % 2 == 0:
            x2, h = _post_attn(xt, a2, b2, lw, i, route=False)
            y = _ffn(h, x2, *dense_w, j)
            if last:
                out = _final_norm(y, final_norm[None])
        else:
            x2, h, meta, gates, counts = _post_attn(xt, a2, b2, lw, i, route=True)
            dest, pends, block_expert, n_valid, n_rows = _moe_layout(meta, counts)
            xs = _dispatch(h, dest, pends, n_rows)
            ys = _moe_ffn(xs, *moe_w, j, block_expert, n_valid)
            y = _combine(x2, ys, gates, dest, final_norm[None] if last else None)
            if last:
                out = y
        x = y.reshape(B, S, D)
    return out.reshape(B, S, D)
```
